```python
import math
import jax, jax.numpy as jnp
from jax import lax
import numpy as np

D_MODEL = 1024
BATCH = 2
SEQ = 8192
DEPTH = 4
DEC_BATCH = 128
DEC_SEQ = 1
PAST_LEN = 8192
PAGE_SIZE = 128

N_MIXERS = 2
N_MLA_LAYERS = (DEPTH + N_MIXERS - 1) // N_MIXERS
N_SB_LAYERS = DEPTH // N_MIXERS
MLA_HEADS = 8
MLA_Q_LORA = 384
MLA_KV_LORA = 256
MLA_NOPE = 128
MLA_ROPE = 64
MLA_V = 128
ROPE_BASE = 10000.0
SB_HEADS = 8
SB_KV_HEADS = 2
SB_GROUP = SB_HEADS // SB_KV_HEADS
SB_HEAD_DIM = 128
D_FF = 2816
CONV_WIDTH = 3
Q_BLOCK = 128
POOL_NUM = 5
POOL_DEN = 4
RMS_EPS = 1e-6

kernel_name = "hybrid_mla_stickbreaking_convffn_step"


def rmsnorm(x, g):
    xf = x.astype(jnp.float32)
    y = xf * lax.rsqrt(jnp.mean(xf * xf, axis=-1, keepdims=True) + RMS_EPS)
    return (y * g.astype(jnp.float32)).astype(x.dtype)


def rope(x, pos):
    half = x.shape[-1] // 2
    inv = ROPE_BASE ** (-jnp.arange(half, dtype=jnp.float32) / half)
    ang = pos.astype(jnp.float32)[:, None] * inv[None, :]
    ang = ang.reshape((1, pos.shape[0]) + (1,) * (x.ndim - 3) + (half,))
    cos, sin = jnp.cos(ang), jnp.sin(ang)
    xf = x.astype(jnp.float32)
    x1, x2 = xf[..., :half], xf[..., half:]
    return jnp.concatenate([x1 * cos - x2 * sin, x2 * cos + x1 * sin], axis=-1).astype(x.dtype)


def gather_pages(pool, page_table):
    g = pool[page_table]
    return g.reshape((g.shape[0], g.shape[1] * g.shape[2]) + g.shape[3:])


def sweep_query_blocks(attend, q_parts, q_pos):
    t = q_pos.shape[0]
    nb = t // Q_BLOCK
    blocks = tuple(jnp.moveaxis(q.reshape((q.shape[0], nb, Q_BLOCK) + q.shape[2:]), 1, 0)
                   for q in q_parts)
    pos_b = q_pos.reshape(nb, Q_BLOCK)
    out = lax.map(lambda a: attend(*a), blocks + (pos_b,))
    out = jnp.moveaxis(out, 0, 1)
    return out.reshape((out.shape[0], t) + out.shape[3:])


def mla_project(h, pos, w_dq, g_q, w_uq, w_dkv, g_kv, w_uk):
    c_q = rmsnorm(h @ w_dq, g_q)
    q = jnp.einsum('btc,chd->bthd', c_q, w_uq)
    q_nope = q[..., :MLA_NOPE]
    q_pe = rope(q[..., MLA_NOPE:], pos)
    q_lat = jnp.einsum('bthn,chn->bthc', q_nope, w_uk)
    kv = h @ w_dkv
    ckv = rmsnorm(kv[..., :MLA_KV_LORA], g_kv)
    kpe = rope(kv[..., MLA_KV_LORA:], pos)
    return q_lat, q_pe, ckv, kpe


def mla_attend(q_lat, q_pe, ckv, kpe, q_pos, k_pos):
    scale = 1.0 / math.sqrt(MLA_NOPE + MLA_ROPE)
    s = (jnp.einsum('bqhc,bkc->bhqk', q_lat, ckv)
         + jnp.einsum('bqhr,bkr->bhqk', q_pe, kpe)).astype(jnp.float32) * scale
    s = jnp.where(k_pos[None, :] <= q_pos[:, None], s, -jnp.inf)
    p = jax.nn.softmax(s, axis=-1)
    return jnp.einsum('bhqk,bkc->bqhc', p.astype(ckv.dtype), ckv)


def mla_output(o_lat, w_uv, w_o):
    o = jnp.einsum('bthc,chv->bthv', o_lat, w_uv)
    return o.reshape(o.shape[0], o.shape[1], -1) @ w_o


def sb_project(h, w_qkv):
    qkv = h @ w_qkv
    b, t = h.shape[0], h.shape[1]
    nq = SB_HEADS * SB_HEAD_DIM
    nk = SB_KV_HEADS * SB_HEAD_DIM
    q = qkv[..., :nq].reshape(b, t, SB_KV_HEADS, SB_GROUP, SB_HEAD_DIM)
    k = qkv[..., nq:nq + nk].reshape(b, t, SB_KV_HEADS, SB_HEAD_DIM)
    v = qkv[..., nq + nk:].reshape(b, t, SB_KV_HEADS, SB_HEAD_DIM)
    return q, k, v


def sb_attend(q, k, v, q_pos, k_pos):
    z = jnp.einsum('bqhgd,bkhd->bhgqk', q, k).astype(jnp.float32) / math.sqrt(SB_HEAD_DIM)
    mask = k_pos[None, :] < q_pos[:, None]
    log_beta = jax.nn.log_sigmoid(z)
    log_keep = jnp.where(mask, jax.nn.log_sigmoid(-z), 0.0)
    suffix = lax.cumsum(log_keep, axis=log_keep.ndim - 1, reverse=True) - log_keep
    a = jnp.where(mask, jnp.exp(log_beta + suffix), 0.0)
    return jnp.einsum('bhgqk,bkhd->bqhgd', a.astype(v.dtype), v)


def sb_output(o, w_o):
    return o.reshape(o.shape[0], o.shape[1], -1) @ w_o


def conv_ffn(h, prev, w_up, conv_w, conv_b, w_down):
    t = h.shape[1]
    gu = h @ w_up
    g, u = gu[..., :D_FF], gu[..., D_FF:]
    gp = jnp.concatenate([prev.astype(g.dtype), g], axis=1)
    gc = conv_b + sum(gp[:, k:k + t] * conv_w[k] for k in range(CONV_WIDTH))
    y = (jax.nn.silu(gc) * u) @ w_down
    return y, gp[:, t:]


def setup_inputs(seed: int = 0) -> dict:
    key = jax.random.key(seed)
    ks = jax.random.split(key, 32)
    n_pages = PAST_LEN // PAGE_SIZE
    n_used = DEC_BATCH * n_pages
    n_pool = (n_used * POOL_NUM) // POOL_DEN
    nrm = jax.random.normal
    f32 = jnp.float32

    def w(k, shape, fan_in):
        return nrm(k, shape, f32) * (fan_in ** -0.5)

    def gain(k, shape):
        return 1.0 + 0.02 * nrm(k, shape, f32)

    page_table = jax.random.permutation(ks[0], n_pool)[:n_used].reshape(DEC_BATCH, n_pages).astype(jnp.int32)
    sb_qkv_out = (SB_HEADS + 2 * SB_KV_HEADS) * SB_HEAD_DIM
    return {
        "x_prompt": nrm(ks[1], (BATCH, SEQ, D_MODEL), f32),
        "x_sample": nrm(ks[2], (DEC_BATCH, DEC_SEQ, D_MODEL), f32),
        "cache_mla_ckv": nrm(ks[3], (N_MLA_LAYERS, n_pool, PAGE_SIZE, MLA_KV_LORA), f32),
        "cache_mla_kpe": nrm(ks[4], (N_MLA_LAYERS, n_pool, PAGE_SIZE, MLA_ROPE), f32),
        "cache_sb_k": nrm(ks[5], (N_SB_LAYERS, n_pool, PAGE_SIZE, SB_KV_HEADS, SB_HEAD_DIM), f32),
        "cache_sb_v": nrm(ks[6], (N_SB_LAYERS, n_pool, PAGE_SIZE, SB_KV_HEADS, SB_HEAD_DIM), f32),
        "state_ffn_conv": nrm(ks[7], (DEPTH, DEC_BATCH, CONV_WIDTH - 1, D_FF), f32),
        "page_table": page_table,
        "norm_mix": gain(ks[8], (DEPTH, D_MODEL)),
        "norm_ffn": gain(ks[9], (DEPTH, D_MODEL)),
        "norm_final": gain(ks[10], (D_MODEL,)),
        "mla_w_dq": w(ks[11], (N_MLA_LAYERS, D_MODEL, MLA_Q_LORA), D_MODEL),
        "mla_g_q": gain(ks[12], (N_MLA_LAYERS, MLA_Q_LORA)),
        "mla_w_uq": w(ks[13], (N_MLA_LAYERS, MLA_Q_LORA, MLA_HEADS, MLA_NOPE + MLA_ROPE), MLA_Q_LORA),
        "mla_w_dkv": w(ks[14], (N_MLA_LAYERS, D_MODEL, MLA_KV_LORA + MLA_ROPE), D_MODEL),
        "mla_g_kv": gain(ks[15], (N_MLA_LAYERS, MLA_KV_LORA)),
        "mla_w_uk": w(ks[16], (N_MLA_LAYERS, MLA_KV_LORA, MLA_HEADS, MLA_NOPE), MLA_KV_LORA),
        "mla_w_uv": w(ks[17], (N_MLA_LAYERS, MLA_KV_LORA, MLA_HEADS, MLA_V), MLA_KV_LORA),
        "mla_w_o": w(ks[18], (N_MLA_LAYERS, MLA_HEADS * MLA_V, D_MODEL), MLA_HEADS * MLA_V),
        "sb_w_qkv": w(ks[19], (N_SB_LAYERS, D_MODEL, sb_qkv_out), D_MODEL),
        "sb_w_o": w(ks[20], (N_SB_LAYERS, SB_HEADS * SB_HEAD_DIM, D_MODEL), SB_HEADS * SB_HEAD_DIM),
        "ffn_w_up": w(ks[21], (DEPTH, D_MODEL, 2 * D_FF), D_MODEL),
        "ffn_conv_w": w(ks[22], (DEPTH, CONV_WIDTH, D_FF), CONV_WIDTH),
        "ffn_conv_b": 0.01 * nrm(ks[23], (DEPTH, D_FF), f32),
        "ffn_w_down": w(ks[24], (DEPTH, D_FF, D_MODEL), D_FF),
    }


def reference(x_prompt, x_sample, cache_mla_ckv, cache_mla_kpe, cache_sb_k, cache_sb_v,
              state_ffn_conv, page_table, norm_mix, norm_ffn, norm_final,
              mla_w_dq, mla_g_q, mla_w_uq, mla_w_dkv, mla_g_kv, mla_w_uk, mla_w_uv, mla_w_o,
              sb_w_qkv, sb_w_o, ffn_w_up, ffn_conv_w, ffn_conv_b, ffn_w_down):
    xp, xs = x_prompt, x_sample
    tp, ts = xp.shape[1], xs.shape[1]
    pos_p = jnp.arange(tp, dtype=jnp.int32)
    pos_s = PAST_LEN + jnp.arange(ts, dtype=jnp.int32)
    kpos_s = jnp.arange(PAST_LEN + ts, dtype=jnp.int32)

    ckv_p_l, kpe_p_l, ckv_s_l, kpe_s_l = [], [], [], []
    sbk_p_l, sbv_p_l, sbk_s_l, sbv_s_l = [], [], [], []
    conv_p_l, conv_s_l = [], []

    for i in range(DEPTH):
        j = i // N_MIXERS
        hp = rmsnorm(xp, norm_mix[i])
        hs = rmsnorm(xs, norm_mix[i])
        if i % N_MIXERS == 0:
            prm = (mla_w_dq[j], mla_g_q[j], mla_w_uq[j], mla_w_dkv[j], mla_g_kv[j], mla_w_uk[j])
            ql_p, qpe_p, ckv_p, kpe_p = mla_project(hp, pos_p, *prm)
            o_p = sweep_query_blocks(
                lambda ql, qr, qpos: mla_attend(ql, qr, ckv_p, kpe_p, qpos, pos_p),
                (ql_p, qpe_p), pos_p)
            ql_s, qpe_s, ckv_s, kpe_s = mla_project(hs, pos_s, *prm)
            ckv_all = jnp.concatenate([gather_pages(cache_mla_ckv[j], page_table), ckv_s], axis=1)
            kpe_all = jnp.concatenate([gather_pages(cache_mla_kpe[j], page_table), kpe_s], axis=1)
            o_s = mla_attend(ql_s, qpe_s, ckv_all, kpe_all, pos_s, kpos_s)
            xp = xp + mla_output(o_p, mla_w_uv[j], mla_w_o[j])
            xs = xs + mla_output(o_s, mla_w_uv[j], mla_w_o[j])
            ckv_p_l.append(ckv_p); kpe_p_l.append(kpe_p)
            ckv_s_l.append(ckv_s); kpe_s_l.append(kpe_s)
        else:
            q_p, k_p, v_p = sb_project(hp, sb_w_qkv[j])
            o_p = sweep_query_blocks(
                lambda qb, qpos: sb_attend(qb, k_p, v_p, qpos, pos_p),
                (q_p,), pos_p)
            q_s, k_s, v_s = sb_project(hs, sb_w_qkv[j])
            k_all = jnp.concatenate([gather_pages(cache_sb_k[j], page_table), k_s], axis=1)
            v_all = jnp.concatenate([gather_pages(cache_sb_v[j], page_table), v_s], axis=1)
            o_s = sb_attend(q_s, k_all, v_all, pos_s, kpos_s)
            xp = xp + sb_output(o_p, sb_w_o[j])
            xs = xs + sb_output(o_s, sb_w_o[j])
            sbk_p_l.append(k_p); sbv_p_l.append(v_p)
            sbk_s_l.append(k_s); sbv_s_l.append(v_s)

        hp = rmsnorm(xp, norm_ffn[i])
        hs = rmsnorm(xs, norm_ffn[i])
        zeros_prev = jnp.zeros((hp.shape[0], CONV_WIDTH - 1, D_FF), hp.dtype)
        yp, cp = conv_ffn(hp, zeros_prev, ffn_w_up[i], ffn_conv_w[i], ffn_conv_b[i], ffn_w_down[i])
        ys, cs = conv_ffn(hs, state_ffn_conv[i], ffn_w_up[i], ffn_conv_w[i], ffn_conv_b[i], ffn_w_down[i])
        xp = xp + yp
        xs = xs + ys
        conv_p_l.append(cp); conv_s_l.append(cs)

    y_prompt = rmsnorm(xp, norm_final)
    y_sample = rmsnorm(xs, norm_final)
    return (y_prompt, y_sample,
            jnp.stack(ckv_p_l), jnp.stack(kpe_p_l), jnp.stack(sbk_p_l), jnp.stack(sbv_p_l),
            jnp.stack(conv_p_l),
            jnp.stack(ckv_s_l), jnp.stack(kpe_s_l), jnp.stack(sbk_s_l), jnp.stack(sbv_s_l),
            jnp.stack(conv_s_l))
```

```python
import functools
import math

import jax
import jax.numpy as jnp
from jax import lax
from jax.experimental import pallas as pl
from jax.experimental.pallas import tpu as pltpu

F32 = jnp.float32
BF16 = jnp.bfloat16
RMS_EPS = 1e-6
ROPE_BASE = 10000.0
LANE = 128
SUBLANE = 8
VMEM_LIMIT = 56 * 1024 * 1024
SB_UNDERFLOW = -104.0
_NT = (((1,), (1,)), ((), ()))


def _params(*semantics):
    return pltpu.CompilerParams(dimension_semantics=semantics, vmem_limit_bytes=VMEM_LIMIT)


def _const_spec(shape):
    nd = len(shape)
    return pl.BlockSpec(shape, lambda *_: (0,) * nd, pipeline_mode=pl.Buffered(1))


def _rmsnorm(x, g):
    return x * lax.rsqrt(jnp.mean(x * x, axis=-1, keepdims=True) + RMS_EPS) * g


def _dot(a, b):
    return jnp.dot(a, b, preferred_element_type=F32)


def _dot_nt(a, b):
    return lax.dot_general(a, b, _NT, preferred_element_type=F32)


def _mla_proj_kernel(x_ref, cos_ref, sin_ref, gmix_ref, w1_ref, gq_ref, gkv_ref, w2_ref,
                     wukt_ref, qcat_ref, kcat_ref, ckv_ref, kpe_ref, *, heads, q_lora,
                     kv_lora, nope, rope):
    scale = 1.0 / math.sqrt(nope + rope)
    h = _rmsnorm(x_ref[...], gmix_ref[...]).astype(BF16)
    y = _dot(h, w1_ref[...])
    c_q = _rmsnorm(y[:, :q_lora], gq_ref[...]).astype(BF16)
    ckv = _rmsnorm(y[:, q_lora:q_lora + kv_lora], gkv_ref[...])
    cos = cos_ref[...]
    sin = sin_ref[...]
    r0 = q_lora + kv_lora
    kpe = y[:, r0:r0 + rope] * cos[:, :rope] + y[:, r0 + LANE:r0 + LANE + rope] * sin[:, :rope]
    ckv_ref[...] = ckv
    kpe_ref[...] = kpe
    kcat_ref[:, :kv_lora] = ckv.astype(BF16)
    kcat_ref[:, kv_lora:] = kpe.astype(BF16)

    q = _dot(c_q, w2_ref[...])
    n0 = heads * nope
    n1 = n0 + heads * rope
    reps = heads * rope // LANE
    cos_h = jnp.concatenate([cos] * reps, axis=-1)
    sin_h = jnp.concatenate([sin] * reps, axis=-1)
    qpe = (q[:, n0:n1] * cos_h + q[:, n1:] * sin_h) * scale
    for hh in range(heads):
        qn = q[:, hh * nope:(hh + 1) * nope].astype(BF16)
        qlat = _dot(qn, wukt_ref[hh]) * scale
        qcat_ref[hh, :, :kv_lora] = qlat.astype(BF16)
        qcat_ref[hh, :, kv_lora:] = qpe[:, hh * rope:(hh + 1) * rope].astype(BF16)


def _mla_project(x2d, cos, sin, gmix, w1, gq, gkv, w2, wukt, *, batch, seq, tm):
    d_model = x2d.shape[1]
    heads, nope, kv_lora = wukt.shape
    q_lora = gq.shape[1]
    rope = (w2.shape[1] - heads * nope) // (2 * heads)
    kdim = kv_lora + rope
    nt = seq // tm
    rows = batch * seq
    kern = functools.partial(_mla_proj_kernel, heads=heads, q_lora=q_lora, kv_lora=kv_lora,
                             nope=nope, rope=rope)
    return pl.pallas_call(
        kern,
        grid=(batch * nt,),
        in_specs=[
            pl.BlockSpec((tm, d_model), lambda i: (i, 0)),
            pl.BlockSpec((tm, LANE), lambda i: (i % nt, 0)),
            pl.BlockSpec((tm, LANE), lambda i: (i % nt, 0)),
            _const_spec(gmix.shape), _const_spec(w1.shape), _const_spec(gq.shape),
            _const_spec(gkv.shape), _const_spec(w2.shape), _const_spec(wukt.shape),
        ],
        out_specs=[
            pl.BlockSpec((None, heads, tm, kdim), lambda i: (i // nt, 0, i % nt, 0)),
            pl.BlockSpec((None, tm, kdim), lambda i: (i // nt, i % nt, 0)),
            pl.BlockSpec((tm, kv_lora), lambda i: (i, 0)),
            pl.BlockSpec((tm, rope), lambda i: (i, 0)),
        ],
        out_shape=[
            jax.ShapeDtypeStruct((batch, heads, seq, kdim), BF16),
            jax.ShapeDtypeStruct((batch, seq, kdim), BF16),
            jax.ShapeDtypeStruct((rows, kv_lora), F32),
            jax.ShapeDtypeStruct((rows, rope), F32),
        ],
        compiler_params=_params("arbitrary"),
        name="mla_project",
    )(x2d, cos, sin, gmix, w1, gq, gkv, w2, wukt)


def _mla_flash_kernel(q_ref, k_ref, o_ref, m_ref, l_ref, acc_ref, *, heads, tq, tk, kv_lora):
    qi = pl.program_id(1)
    rows = heads * tq
    q = q_ref[...].reshape(rows, q_ref.shape[-1])
    m_ref[...] = jnp.full(m_ref.shape, -jnp.inf, F32)
    l_ref[...] = jnp.zeros(l_ref.shape, F32)
    acc_ref[...] = jnp.zeros(acc_ref.shape, F32)

    def block(j, masked):
        start = pl.multiple_of(j * tk, tk)
        k = k_ref[pl.ds(start, tk), :]
        s = _dot_nt(q, k)
        if masked:
            tok = qi * tq + lax.broadcasted_iota(jnp.int32, (tq, tk), 0)
            key = start + lax.broadcasted_iota(jnp.int32, (tq, tk), 1)
            s = jnp.where((key <= tok)[None], s.reshape(heads, tq, tk), -jnp.inf)
            s = s.reshape(rows, tk)
        m_prev = m_ref[...]
        m_new = jnp.maximum(m_prev, jnp.max(s, axis=-1, keepdims=True))
        alpha = jnp.exp(m_prev - m_new)
        p = jnp.exp(s - m_new)
        l_ref[...] = alpha * l_ref[...] + jnp.sum(p, axis=-1, keepdims=True)
        acc_ref[...] = alpha * acc_ref[...] + _dot(p.astype(BF16), k[:, :kv_lora])
        m_ref[...] = m_new

    n_full = (qi * tq) // tk

    def body(j, carry):
        block(j, False)
        return carry

    lax.fori_loop(0, n_full, body, 0)
    block(n_full, True)
    o = acc_ref[...] / l_ref[...]
    o_ref[...] = o.reshape(heads, tq, kv_lora).astype(o_ref.dtype)


def _mla_flash(qcat, kcat, *, kv_lora, tq, tk):
    batch, heads, seq, kdim = qcat.shape
    kern = functools.partial(_mla_flash_kernel, heads=heads, tq=tq, tk=tk, kv_lora=kv_lora)
    rows = heads * tq
    return pl.pallas_call(
        kern,
        grid=(batch, seq // tq),
        in_specs=[
            pl.BlockSpec((None, heads, tq, kdim), lambda b, i: (b, 0, i, 0)),
            pl.BlockSpec((None, seq, kdim), lambda b, i: (b, 0, 0)),
        ],
        out_specs=pl.BlockSpec((None, heads, tq, kv_lora), lambda b, i: (b, 0, i, 0)),
        out_shape=jax.ShapeDtypeStruct((batch, heads, seq, kv_lora), BF16),
        scratch_shapes=[pltpu.VMEM((rows, 1), F32), pltpu.VMEM((rows, 1), F32),
                        pltpu.VMEM((rows, kv_lora), F32)],
        compiler_params=_params("arbitrary", "arbitrary"),
        name="mla_flash",
    )(qcat, kcat)


def _mla_decode_kernel(pt_ref, q_ref, knew_ref, ckv_hbm, kpe_hbm, o_ref, cbuf, rbuf, sem,
                       s_ref, *, layer, n_pages, page, kv_lora, chunk_pages):
    b = pl.program_id(0)
    nb = pl.num_programs(0)
    slot = lax.rem(b, 2)

    def page_copies(seq, slot_, p):
        pg = pt_ref[seq * n_pages + p]
        return (pltpu.make_async_copy(ckv_hbm.at[layer, pg], cbuf.at[slot_, p], sem.at[0, slot_]),
                pltpu.make_async_copy(kpe_hbm.at[layer, pg], rbuf.at[slot_, p], sem.at[1, slot_]))

    def start_seq(seq, slot_):
        def body(p, carry):
            for c in page_copies(seq, slot_, p):
                c.start()
            return carry
        lax.fori_loop(0, n_pages, body, 0)

    def wait_seq(seq, slot_):
        def body(p, carry):
            for c in page_copies(seq, slot_, p):
                c.wait()
            return carry
        lax.fori_loop(0, n_pages, body, 0)

    @pl.when(b == 0)
    def _():
        start_seq(0, 0)

    @pl.when(b + 1 < nb)
    def _():
        start_seq(b + 1, 1 - slot)

    wait_seq(b, slot)

    q = q_ref[0]
    q_lat = q[:, :kv_lora]
    q_pe = q[:, kv_lora:]
    ck = chunk_pages * page
    n_chunks = n_pages // chunk_pages

    def ckv_chunk(c):
        lo = c * chunk_pages
        return cbuf[slot, lo:lo + chunk_pages].reshape(ck, kv_lora).astype(BF16)

    def kpe_t_chunk(c):
        lo = c * chunk_pages
        return jnp.concatenate([rbuf[slot, lo + i] for i in range(chunk_pages)],
                               axis=-1).astype(BF16)

    for c in range(n_chunks):
        s_ref[:, c * ck:(c + 1) * ck] = _dot_nt(q_lat, ckv_chunk(c)) + _dot(q_pe, kpe_t_chunk(c))

    knew = knew_ref[0]
    s_new = jnp.sum(q.astype(F32) * knew.astype(F32), axis=-1, keepdims=True)
    s_all = s_ref[...]
    m = jnp.maximum(jnp.max(s_all, axis=-1, keepdims=True), s_new)
    p_all = jnp.exp(s_all - m)
    p_new = jnp.exp(s_new - m)
    denom = jnp.sum(p_all, axis=-1, keepdims=True) + p_new
    s_ref[...] = p_all

    acc = p_new.astype(BF16).astype(F32) * knew[:, :kv_lora].astype(F32)
    for c in range(n_chunks):
        acc = acc + _dot(s_ref[:, c * ck:(c + 1) * ck].astype(BF16), ckv_chunk(c))
    o_ref[0] = (acc / denom).astype(o_ref.dtype)


def _mla_decode(page_table_flat, q_s, knew, cache_ckv, cache_kpe_t, *, layer, chunk_pages=8):
    n_seq, heads, kdim = q_s.shape
    _, _, page, kv_lora = cache_ckv.shape
    rope = cache_kpe_t.shape[2]
    n_pages = page_table_flat.shape[0] // n_seq
    chunk_pages = math.gcd(chunk_pages, n_pages)
    kern = functools.partial(_mla_decode_kernel, layer=layer, n_pages=n_pages, page=page,
                             kv_lora=kv_lora, chunk_pages=chunk_pages)
    grid_spec = pltpu.PrefetchScalarGridSpec(
        num_scalar_prefetch=1,
        grid=(n_seq,),
        in_specs=[
            pl.BlockSpec((1, heads, kdim), lambda b, pt: (b, 0, 0)),
            pl.BlockSpec((1, 1, kdim), lambda b, pt: (b, 0, 0)),
            pl.BlockSpec(memory_space=pl.ANY),
            pl.BlockSpec(memory_space=pl.ANY),
        ],
        out_specs=pl.BlockSpec((1, heads, kv_lora), lambda b, pt: (b, 0, 0)),
        scratch_shapes=[
            pltpu.VMEM((2, n_pages, page, kv_lora), F32),
            pltpu.VMEM((2, n_pages, rope, page), F32),
            pltpu.SemaphoreType.DMA((2, 2)),
            pltpu.VMEM((heads, n_pages * page), F32),
        ],
    )
    return pl.pallas_call(
        kern,
        grid_spec=grid_spec,
        out_shape=jax.ShapeDtypeStruct((n_seq, heads, kv_lora), BF16),
        compiler_params=_params("arbitrary"),
        name="mla_decode",
    )(page_table_flat, q_s, knew, cache_ckv, cache_kpe_t)


def _mla_out_kernel(o_ref, x_ref, wuv_ref, wo_ref, out_ref, *, heads):
    parts = [_dot(o_ref[hh], wuv_ref[hh]).astype(BF16) for hh in range(heads)]
    o = jnp.concatenate(parts, axis=-1)
    out_ref[...] = x_ref[...] + _dot(o, wo_ref[...])


def _mla_out(o_lat, x2d, wuv, wo, *, tm):
    batch, heads, seq, kv_lora = o_lat.shape
    d_model = x2d.shape[1]
    nt = seq // tm
    return pl.pallas_call(
        functools.partial(_mla_out_kernel, heads=heads),
        grid=(batch * nt,),
        in_specs=[
            pl.BlockSpec((None, heads, tm, kv_lora), lambda i: (i // nt, 0, i % nt, 0)),
            pl.BlockSpec((tm, d_model), lambda i: (i, 0)),
            _const_spec(wuv.shape), _const_spec(wo.shape),
        ],
        out_specs=pl.BlockSpec((tm, d_model), lambda i: (i, 0)),
        out_shape=jax.ShapeDtypeStruct(x2d.shape, F32),
        compiler_params=_params("arbitrary"),
        name="mla_out",
    )(o_lat, x2d, wuv, wo)


def _sb_proj_kernel(x_ref, gmix_ref, w_ref, q_ref, kbf_ref, vbf_ref, k_ref, v_ref, *,
                    heads, head_dim):
    h = _rmsnorm(x_ref[...], gmix_ref[...]).astype(BF16)
    y = _dot(h, w_ref[...])
    for hh in range(heads):
        q_ref[hh] = y[:, hh * head_dim:(hh + 1) * head_dim].astype(BF16)
    nq = heads * head_dim
    nk = k_ref.shape[-1]
    k = y[:, nq:nq + nk]
    v = y[:, nq + nk:]
    k_ref[...] = k
    v_ref[...] = v
    kbf_ref[...] = k.astype(BF16)
    vbf_ref[...] = v.astype(BF16)


def _sb_project(x2d, gmix, w, *, batch, seq, tm, heads, head_dim):
    d_model = x2d.shape[1]
    nk = (w.shape[1] - heads * head_dim) // 2
    nt = seq // tm
    rows = batch * seq
    return pl.pallas_call(
        functools.partial(_sb_proj_kernel, heads=heads, head_dim=head_dim),
        grid=(batch * nt,),
        in_specs=[pl.BlockSpec((tm, d_model), lambda i: (i, 0)),
                  _const_spec(gmix.shape), _const_spec(w.shape)],
        out_specs=[
            pl.BlockSpec((None, heads, tm, head_dim), lambda i: (i // nt, 0, i % nt, 0)),
            pl.BlockSpec((None, tm, nk), lambda i: (i // nt, i % nt, 0)),
            pl.BlockSpec((None, tm, nk), lambda i: (i // nt, i % nt, 0)),
            pl.BlockSpec((tm, nk), lambda i: (i, 0)),
            pl.BlockSpec((tm, nk), lambda i: (i, 0)),
        ],
        out_shape=[
            jax.ShapeDtypeStruct((batch, heads, seq, head_dim), BF16),
            jax.ShapeDtypeStruct((batch, seq, nk), BF16),
            jax.ShapeDtypeStruct((batch, seq, nk), BF16),
            jax.ShapeDtypeStruct((rows, nk), F32),
            jax.ShapeDtypeStruct((rows, nk), F32),
        ],
        compiler_params=_params("arbitrary"),
        name="sb_project",
    )(x2d, gmix, w)


def _sb_block(q, k, v, tri, carry, mask, inv_sqrt):
    z = _dot_nt(q, k) * inv_sqrt
    sp = jnp.log1p(jnp.exp(-jnp.abs(z)))
    log_beta = jnp.minimum(z, 0.0) - sp
    log_keep = jnp.minimum(-z, 0.0) - sp
    if mask is not None:
        log_keep = jnp.where(mask, log_keep, 0.0)
    hi = log_keep.astype(BF16)
    lo = (log_keep - hi.astype(F32)).astype(BF16)
    suffix = _dot(hi, tri) + _dot(lo, tri)
    a = jnp.exp(log_beta + suffix + carry)
    if mask is not None:
        a = jnp.where(mask, a, 0.0)
    return _dot(a.astype(BF16), v), jnp.sum(log_keep, axis=-1, keepdims=True)


def _sb_attn_kernel(q_ref, k_ref, v_ref, tri_ref, o_ref, carry_ref, acc_ref, *, group, tq,
                    head_dim):
    qi = pl.program_id(2)
    rows = group * tq
    inv_sqrt = 1.0 / math.sqrt(head_dim)
    q = q_ref[...].reshape(rows, head_dim)
    tri = tri_ref[...]

    row = lax.broadcasted_iota(jnp.int32, (tq, tq), 0)
    col = lax.broadcasted_iota(jnp.int32, (tq, tq), 1)
    mask = jnp.broadcast_to((col < row)[None], (group, tq, tq)).reshape(rows, tq)
    start = pl.multiple_of(qi * tq, tq)
    out, ksum = _sb_block(q, k_ref[pl.ds(start, tq), :], v_ref[pl.ds(start, tq), :], tri,
                          jnp.zeros((rows, 1), F32), mask, inv_sqrt)
    acc_ref[...] = out
    carry_ref[...] = ksum

    def cond(state):
        j, cmax = state
        return jnp.logical_and(j >= 0, cmax > SB_UNDERFLOW)

    def body(state):
        j, _ = state
        st = pl.multiple_of(j * tq, tq)
        carry = carry_ref[...]
        out, ksum = _sb_block(q, k_ref[pl.ds(st, tq), :], v_ref[pl.ds(st, tq), :], tri, carry,
                              None, inv_sqrt)
        acc_ref[...] += out
        carry = carry + ksum
        carry_ref[...] = carry
        return j - 1, jnp.max(carry)

    lax.while_loop(cond, body, (qi - 1, jnp.max(ksum)))
    o_ref[...] = acc_ref[...].reshape(group, tq, head_dim).astype(o_ref.dtype)


def _sb_attention(q, kbf, vbf, tri, *, tq):
    batch, heads, seq, head_dim = q.shape
    kv_heads = kbf.shape[-1] // head_dim
    group = heads // kv_heads
    rows = group * tq
    kern = functools.partial(_sb_attn_kernel, group=group, tq=tq, head_dim=head_dim)
    return pl.pallas_call(
        kern,
        grid=(batch, kv_heads, seq // tq),
        in_specs=[
            pl.BlockSpec((None, group, tq, head_dim), lambda b, h, i: (b, h, i, 0)),
            pl.BlockSpec((None, seq, head_dim), lambda b, h, i: (b, 0, h)),
            pl.BlockSpec((None, seq, head_dim), lambda b, h, i: (b, 0, h)),
            _const_spec(tri.shape),
        ],
        out_specs=pl.BlockSpec((None, group, tq, head_dim), lambda b, h, i: (b, h, i, 0)),
        out_shape=jax.ShapeDtypeStruct(q.shape, BF16),
        scratch_shapes=[pltpu.VMEM((rows, 1), F32), pltpu.VMEM((rows, head_dim), F32)],
        compiler_params=_params("arbitrary", "arbitrary", "arbitrary"),
        name="sb_attention",
    )(q, kbf, vbf, tri)


def _sb_decode_kernel(pt_ref, q_ref, tri_ref, k_hbm, v_hbm, o_ref, kbuf, vbuf, sem, *, layer,
                      n_pages, kv_heads, group, head_dim):
    b = pl.program_id(0)
    heads = kv_heads * group
    inv_sqrt = 1.0 / math.sqrt(head_dim)

    def copies(p, slot_):
        pg = pt_ref[b * n_pages + p]
        return (pltpu.make_async_copy(k_hbm.at[layer, pg], kbuf.at[slot_], sem.at[0, slot_]),
                pltpu.make_async_copy(v_hbm.at[layer, pg], vbuf.at[slot_], sem.at[1, slot_]))

    def slot_of(p):
        return lax.rem(n_pages - 1 - p, 2)

    for c in copies(n_pages - 1, 0):
        c.start()

    q = q_ref[0]
    tri = tri_ref[...]
    width = kbuf.shape[1]
    kv_of_row = lax.broadcasted_iota(jnp.int32, (heads, width), 0) // group
    kv_of_col = lax.rem(lax.broadcasted_iota(jnp.int32, (heads, width), 1), kv_heads)
    mask = kv_of_row == kv_of_col

    def cond(state):
        p, cmax, _, _ = state
        return jnp.logical_and(p >= 0, cmax > SB_UNDERFLOW)

    def body(state):
        p, _, carry, acc = state
        slot_ = slot_of(p)
        for c in copies(p, slot_):
            c.wait()

        @pl.when(p > 0)
        def _():
            for c in copies(p - 1, 1 - slot_):
                c.start()

        out, ksum = _sb_block(q, kbuf[slot_].astype(BF16), vbuf[slot_].astype(BF16), tri, carry,
                              mask, inv_sqrt)
        carry = carry + ksum
        return p - 1, jnp.max(carry), carry, acc + out

    init = (jnp.int32(n_pages - 1), jnp.float32(0.0), jnp.zeros((heads, 1), F32),
            jnp.zeros((heads, head_dim), F32))
    p_end, _, _, acc = lax.while_loop(cond, body, init)

    @pl.when(p_end >= 0)
    def _():
        for c in copies(p_end, slot_of(p_end)):
            c.wait()

    o_ref[0] = acc.astype(o_ref.dtype)


def _sb_decode(page_table_flat, q_s, tri, cache_k, cache_v, *, layer, kv_heads):
    n_seq, heads, head_dim = q_s.shape
    width = cache_k.shape[2]
    n_pages = page_table_flat.shape[0] // n_seq
    kern = functools.partial(_sb_decode_kernel, layer=layer, n_pages=n_pages, kv_heads=kv_heads,
                             group=heads // kv_heads, head_dim=head_dim)
    grid_spec = pltpu.PrefetchScalarGridSpec(
        num_scalar_prefetch=1,
        grid=(n_seq,),
        in_specs=[
            pl.BlockSpec((1, heads, head_dim), lambda b, pt: (b, 0, 0)),
            pl.BlockSpec(tri.shape, lambda b, pt: (0, 0)),
            pl.BlockSpec(memory_space=pl.ANY),
            pl.BlockSpec(memory_space=pl.ANY),
        ],
        out_specs=pl.BlockSpec((1, heads, head_dim), lambda b, pt: (b, 0, 0)),
        scratch_shapes=[
            pltpu.VMEM((2, width, head_dim), F32),
            pltpu.VMEM((2, width, head_dim), F32),
            pltpu.SemaphoreType.DMA((2, 2)),
        ],
    )
    return pl.pallas_call(
        kern,
        grid_spec=grid_spec,
        out_shape=jax.ShapeDtypeStruct((n_seq, heads, head_dim), BF16),
        compiler_params=_params("arbitrary"),
        name="sb_decode",
    )(page_table_flat, q_s, tri, cache_k, cache_v)


def _sb_out_kernel(o_ref, x_ref, wo_ref, out_ref, *, heads):
    o = jnp.concatenate([o_ref[hh] for hh in range(heads)], axis=-1)
    out_ref[...] = x_ref[...] + _dot(o, wo_ref[...])


def _sb_out(o, x2d, wo, *, tm):
    batch, heads, seq, head_dim = o.shape
    d_model = x2d.shape[1]
    nt = seq // tm
    return pl.pallas_call(
        functools.partial(_sb_out_kernel, heads=heads),
        grid=(batch * nt,),
        in_specs=[
            pl.BlockSpec((None, heads, tm, head_dim), lambda i: (i // nt, 0, i % nt, 0)),
            pl.BlockSpec((tm, d_model), lambda i: (i, 0)),
            _const_spec(wo.shape),
        ],
        out_specs=pl.BlockSpec((tm, d_model), lambda i: (i, 0)),
        out_shape=jax.ShapeDtypeStruct(x2d.shape, F32),
        compiler_params=_params("arbitrary"),
        name="sb_out",
    )(o, x2d, wo)


def _ffn_gate(g2, g1, g, u, cw_ref, cb_ref, cols):
    w0 = cw_ref[0:1, cols]
    w1 = cw_ref[1:2, cols]
    w2 = cw_ref[2:3, cols]
    gc = cb_ref[:, cols] + ((g2 * w0 + g1 * w1) + g * w2)
    return (gc * jax.nn.sigmoid(gc) * u).astype(BF16)


def _ffn_prompt_kernel(x_ref, gn_ref, wup_ref, cw_ref, cb_ref, wdn_ref, out_ref, tail_ref,
                       act_ref, carry_ref, *, d_ff, tf, tiles_per_seq):
    i = pl.program_id(0)
    x = x_ref[...]
    tm = x.shape[0]
    h = _rmsnorm(x, gn_ref[...]).astype(BF16)

    @pl.when(i % tiles_per_seq == 0)
    def _():
        carry_ref[...] = jnp.zeros(carry_ref.shape, F32)

    row = lax.broadcasted_iota(jnp.int32, (SUBLANE, tf), 0)
    for c in range(d_ff // tf):
        cols = slice(c * tf, (c + 1) * tf)
        g = _dot(h, wup_ref[:, cols])
        u = _dot(h, wup_ref[:, d_ff + c * tf:d_ff + (c + 1) * tf])
        prev = carry_ref[:, cols]
        r1 = pltpu.roll(g, 1, 0)
        r2 = pltpu.roll(g, 2, 0)
        head1 = jnp.where(row < 1, pltpu.roll(prev, 1, 0), r1[:SUBLANE])
        head2 = jnp.where(row < 2, pltpu.roll(prev, 2, 0), r2[:SUBLANE])
        g1 = jnp.concatenate([head1, r1[SUBLANE:]], axis=0)
        g2 = jnp.concatenate([head2, r2[SUBLANE:]], axis=0)
        tail = g[tm - SUBLANE:, :]
        carry_ref[:, cols] = tail
        tail_ref[:, cols] = tail
        act_ref[:, cols] = _ffn_gate(g2, g1, g, u, cw_ref, cb_ref, cols)
    out_ref[...] = x + _dot(act_ref[...], wdn_ref[...])


def _ffn_prompt(x2d, gn, wup, cw, cb, wdn, *, batch, seq, tm, tf):
    d_model = x2d.shape[1]
    d_ff = wdn.shape[0]
    nt = seq // tm
    kern = functools.partial(_ffn_prompt_kernel, d_ff=d_ff, tf=tf, tiles_per_seq=nt)
    return pl.pallas_call(
        kern,
        grid=(batch * nt,),
        in_specs=[pl.BlockSpec((tm, d_model), lambda i: (i, 0)),
                  _const_spec(gn.shape), _const_spec(wup.shape), _const_spec(cw.shape),
                  _const_spec(cb.shape), _const_spec(wdn.shape)],
        out_specs=[pl.BlockSpec((tm, d_model), lambda i: (i, 0)),
                   pl.BlockSpec((None, SUBLANE, d_ff), lambda i: (i // nt, 0, 0))],
        out_shape=[jax.ShapeDtypeStruct(x2d.shape, F32),
                   jax.ShapeDtypeStruct((batch, SUBLANE, d_ff), F32)],
        scratch_shapes=[pltpu.VMEM((tm, d_ff), BF16), pltpu.VMEM((SUBLANE, d_ff), F32)],
        compiler_params=_params("arbitrary"),
        name="ffn_prompt",
    )(x2d, gn, wup, cw, cb, wdn)


def _ffn_sample_kernel(x_ref, s0_ref, s1_ref, gn_ref, wup_ref, cw_ref, cb_ref, wdn_ref,
                       out_ref, gate_ref, act_ref, *, d_ff, tf):
    x = x_ref[...]
    h = _rmsnorm(x, gn_ref[...]).astype(BF16)
    for c in range(d_ff // tf):
        cols = slice(c * tf, (c + 1) * tf)
        g = _dot(h, wup_ref[:, cols])
        u = _dot(h, wup_ref[:, d_ff + c * tf:d_ff + (c + 1) * tf])
        gate_ref[:, cols] = g
        act_ref[:, cols] = _ffn_gate(s0_ref[:, cols], s1_ref[:, cols], g, u, cw_ref, cb_ref, cols)
    out_ref[...] = x + _dot(act_ref[...], wdn_ref[...])


def _ffn_sample(x2d, s0, s1, gn, wup, cw, cb, wdn, *, tf):
    rows, d_model = x2d.shape
    d_ff = wdn.shape[0]
    kern = functools.partial(_ffn_sample_kernel, d_ff=d_ff, tf=tf)
    return pl.pallas_call(
        kern,
        grid=(1,),
        in_specs=[_const_spec(x2d.shape), _const_spec(s0.shape), _const_spec(s1.shape),
                  _const_spec(gn.shape), _const_spec(wup.shape), _const_spec(cw.shape),
                  _const_spec(cb.shape), _const_spec(wdn.shape)],
        out_specs=[pl.BlockSpec((rows, d_model), lambda i: (0, 0)),
                   pl.BlockSpec((rows, d_ff), lambda i: (0, 0))],
        out_shape=[jax.ShapeDtypeStruct(x2d.shape, F32),
                   jax.ShapeDtypeStruct((rows, d_ff), F32)],
        scratch_shapes=[pltpu.VMEM((rows, d_ff), BF16)],
        compiler_params=_params("arbitrary"),
        name="ffn_sample",
    )(x2d, s0, s1, gn, wup, cw, cb, wdn)


def _final_norm_kernel(x_ref, g_ref, o_ref):
    o_ref[...] = _rmsnorm(x_ref[...], g_ref[...])


def _final_norm(x2d, g, *, tm):
    rows, d_model = x2d.shape
    return pl.pallas_call(
        _final_norm_kernel,
        grid=(rows // tm,),
        in_specs=[pl.BlockSpec((tm, d_model), lambda i: (i, 0)), _const_spec(g.shape)],
        out_specs=pl.BlockSpec((tm, d_model), lambda i: (i, 0)),
        out_shape=jax.ShapeDtypeStruct(x2d.shape, F32),
        compiler_params=_params("arbitrary"),
        name="final_norm",
    )(x2d, g)


def _tiles(seq, d_ff):
    def fit(pref):
        t = min(pref, seq)
        while seq % t:
            t //= 2
        return t
    tf = 2 * LANE if d_ff % (2 * LANE) == 0 else LANE
    return dict(tm=fit(512), mla_tq=fit(128), mla_tk=fit(512), sb_tq=fit(256), tf=tf)


def _rot_cols(w):
    half = w.shape[-1] // 2
    return jnp.concatenate([-w[..., half:], w[..., :half]], axis=-1)


def _rope_tables(pos, rope):
    half = rope // 2
    inv = ROPE_BASE ** (-jnp.arange(half, dtype=F32) / half)
    ang = pos.astype(F32)[:, None] * inv[None, :]
    reps = LANE // half
    return jnp.tile(jnp.cos(ang), (1, reps)), jnp.tile(jnp.sin(ang), (1, reps))


def kernel(x_prompt, x_sample, cache_mla_ckv, cache_mla_kpe, cache_sb_k, cache_sb_v, state_ffn_conv, page_table, norm_mix, norm_ffn, norm_final, mla_w_dq, mla_g_q, mla_w_uq, mla_w_dkv, mla_g_kv, mla_w_uk, mla_w_uv, mla_w_o, sb_w_qkv, sb_w_o, ffn_w_up, ffn_conv_w, ffn_conv_b, ffn_w_down):
    batch, seq, d_model = x_prompt.shape
    n_seq, dec_seq, _ = x_sample.shape
    assert dec_seq == 1, "the sample group decodes one token per sequence"
    depth = norm_mix.shape[0]
    d_ff = ffn_w_down.shape[1]
    n_pages, page = page_table.shape[1], cache_mla_ckv.shape[2]
    past = n_pages * page
    heads, nope = mla_w_uk.shape[2], mla_w_uk.shape[3]
    q_lora, kv_lora = mla_w_dq.shape[2], mla_w_uk.shape[1]
    rope = mla_w_uq.shape[3] - nope
    sb_kv, sb_dim = cache_sb_k.shape[3], cache_sb_k.shape[4]
    sb_heads = sb_w_o.shape[1] // sb_dim
    t = _tiles(seq, d_ff)

    cos_p, sin_p = _rope_tables(jnp.arange(seq, dtype=jnp.int32), rope)
    cos_s, sin_s = _rope_tables(past + jnp.arange(dec_seq, dtype=jnp.int32), rope)
    cos_s = jnp.broadcast_to(cos_s, (n_seq, LANE))
    sin_s = jnp.broadcast_to(sin_s, (n_seq, LANE))
    pt_flat = page_table.reshape(-1)
    cache_k2 = cache_sb_k.reshape(cache_sb_k.shape[:2] + (page * sb_kv, sb_dim))
    cache_v2 = cache_sb_v.reshape(cache_sb_v.shape[:2] + (page * sb_kv, sb_dim))
    cache_kpe_t = jnp.swapaxes(cache_mla_kpe, 2, 3)
    tri_p = jnp.tril(jnp.ones((t["sb_tq"], t["sb_tq"]), BF16), k=-1)
    tri_s = jnp.tril(jnp.ones((page * sb_kv, page * sb_kv), BF16), k=-1)

    xp = x_prompt.reshape(batch * seq, d_model)
    xs = x_sample.reshape(n_seq, d_model)
    row = lambda v: v.reshape(1, -1)
    outs = {k: [] for k in ("ckv_p", "kpe_p", "ckv_s", "kpe_s", "sbk_p", "sbv_p", "sbk_s",
                            "sbv_s", "conv_p", "conv_s")}

    for i in range(depth):
        j = i // 2
        gmix = row(norm_mix[i])
        if i % 2 == 0:
            pad = jnp.zeros((d_model, LANE - rope), F32)
            w_r = mla_w_dkv[j][:, kv_lora:]
            w1 = jnp.concatenate([mla_w_dq[j], mla_w_dkv[j][:, :kv_lora], w_r, pad,
                                  _rot_cols(w_r), pad], axis=1).astype(BF16)
            wq = mla_w_uq[j]
            w2 = jnp.concatenate([wq[:, :, :nope].reshape(q_lora, heads * nope),
                                  wq[:, :, nope:].reshape(q_lora, heads * rope),
                                  _rot_cols(wq[:, :, nope:]).reshape(q_lora, heads * rope)],
                                 axis=1).astype(BF16)
            wukt = jnp.transpose(mla_w_uk[j], (1, 2, 0)).astype(BF16)
            wuv = jnp.transpose(mla_w_uv[j], (1, 0, 2)).astype(BF16)
            wo = mla_w_o[j].astype(BF16)
            prm = (gmix, w1, row(mla_g_q[j]), row(mla_g_kv[j]), w2, wukt)

            qcat, kcat, ckv_p, kpe_p = _mla_project(xp, cos_p, sin_p, *prm, batch=batch,
                                                    seq=seq, tm=t["tm"])
            o_p = _mla_flash(qcat, kcat, kv_lora=kv_lora, tq=t["mla_tq"], tk=t["mla_tk"])
            xp = _mla_out(o_p, xp, wuv, wo, tm=t["tm"])

            qcat_s, kcat_s, ckv_s, kpe_s = _mla_project(xs, cos_s, sin_s, *prm, batch=1,
                                                        seq=n_seq, tm=n_seq)
            q_s = jnp.transpose(qcat_s[0], (1, 0, 2))
            o_s = _mla_decode(pt_flat, q_s, kcat_s.reshape(n_seq, 1, -1), cache_mla_ckv,
                              cache_kpe_t, layer=j)
            xs = _mla_out(jnp.transpose(o_s, (1, 0, 2))[None], xs, wuv, wo, tm=n_seq)

            outs["ckv_p"].append(ckv_p.reshape(batch, seq, kv_lora))
            outs["kpe_p"].append(kpe_p.reshape(batch, seq, rope))
            outs["ckv_s"].append(ckv_s.reshape(n_seq, dec_seq, kv_lora))
            outs["kpe_s"].append(kpe_s.reshape(n_seq, dec_seq, rope))
        else:
            wqkv = sb_w_qkv[j].astype(BF16)
            wo = sb_w_o[j].astype(BF16)
            q_p, kbf, vbf, k_p, v_p = _sb_project(xp, gmix, wqkv, batch=batch, seq=seq,
                                                  tm=t["tm"], heads=sb_heads, head_dim=sb_dim)
            o_p = _sb_attention(q_p, kbf, vbf, tri_p, tq=t["sb_tq"])
            xp = _sb_out(o_p, xp, wo, tm=t["tm"])

            q_s, _, _, k_s, v_s = _sb_project(xs, gmix, wqkv, batch=1, seq=n_seq, tm=n_seq,
                                              heads=sb_heads, head_dim=sb_dim)
            o_s = _sb_decode(pt_flat, jnp.transpose(q_s[0], (1, 0, 2)), tri_s, cache_k2,
                             cache_v2, layer=j, kv_heads=sb_kv)
            xs = _sb_out(jnp.transpose(o_s, (1, 0, 2))[None], xs, wo, tm=n_seq)

            outs["sbk_p"].append(k_p.reshape(batch, seq, sb_kv, sb_dim))
            outs["sbv_p"].append(v_p.reshape(batch, seq, sb_kv, sb_dim))
            outs["sbk_s"].append(k_s.reshape(n_seq, dec_seq, sb_kv, sb_dim))
            outs["sbv_s"].append(v_s.reshape(n_seq, dec_seq, sb_kv, sb_dim))

        gn = row(norm_ffn[i])
        wup = ffn_w_up[i].astype(BF16)
        wdn = ffn_w_down[i].astype(BF16)
        cw, cb = ffn_conv_w[i], row(ffn_conv_b[i])
        xp, tail = _ffn_prompt(xp, gn, wup, cw, cb, wdn, batch=batch, seq=seq, tm=t["tm"],
                               tf=t["tf"])
        st = state_ffn_conv[i]
        xs, gate_s = _ffn_sample(xs, st[:, 0], st[:, 1], gn, wup, cw, cb, wdn, tf=t["tf"])
        outs["conv_p"].append(tail[:, SUBLANE - 2:])
        outs["conv_s"].append(jnp.stack([st[:, 1], gate_s], axis=1))

    gfin = row(norm_final)
    y_p = _final_norm(xp, gfin, tm=t["tm"]).reshape(batch, seq, d_model)
    y_s = _final_norm(xs, gfin, tm=n_seq).reshape(n_seq, dec_seq, d_model)
    st = lambda k: jnp.stack(outs[k])
    return (y_p, y_s, st("ckv_p"), st("kpe_p"), st("sbk_p"), st("sbv_p"), st("conv_p"),
            st("ckv_s"), st("kpe_s"), st("sbk_s"), st("sbv_s"), st("conv_s"))
```

```python
import functools
import math

import jax
import jax.numpy as jnp
from jax import lax
from jax.experimental import pallas as pl
from jax.experimental.pallas import tpu as pltpu

F32 = jnp.float32
BF16 = jnp.bfloat16
RMS_EPS = 1e-6
ROPE_BASE = 10000.0
LANE = 128
SUBLANE = 8
VMEM_LIMIT = 56 * 1024 * 1024
SB_UNDERFLOW = -104.0
_NT = (((1,), (1,)), ((), ()))


def _params(*semantics):
    return pltpu.CompilerParams(dimension_semantics=semantics, vmem_limit_bytes=VMEM_LIMIT)


def _const_spec(shape):
    nd = len(shape)
    return pl.BlockSpec(shape, lambda *_: (0,) * nd, pipeline_mode=pl.Buffered(1))


def _rmsnorm(x, g):
    return x * lax.rsqrt(jnp.mean(x * x, axis=-1, keepdims=True) + RMS_EPS) * g


def _dot(a, b):
    return jnp.dot(a, b, preferred_element_type=F32)


def _dot_nt(a, b):
    return lax.dot_general(a, b, _NT, preferred_element_type=F32)


def _mla_proj_kernel(x_ref, cos_ref, sin_ref, gmix_ref, w1_ref, gq_ref, gkv_ref, w2_ref,
                     wukt_ref, qcat_ref, kcat_ref, ckvt_ref, ckv_ref, kpe_ref, *, heads, q_lora,
                     kv_lora, nope, rope):
    scale = math.log2(math.e) / math.sqrt(nope + rope)
    h = _rmsnorm(x_ref[...], gmix_ref[...]).astype(BF16)
    y = _dot(h, w1_ref[...])
    c_q = _rmsnorm(y[:, :q_lora], gq_ref[...]).astype(BF16)
    ckv = _rmsnorm(y[:, q_lora:q_lora + kv_lora], gkv_ref[...])
    cos = cos_ref[...]
    sin = sin_ref[...]
    r0 = q_lora + kv_lora
    kpe = y[:, r0:r0 + rope] * cos[:, :rope] + y[:, r0 + LANE:r0 + LANE + rope] * sin[:, :rope]
    ckv_ref[...] = ckv
    kpe_ref[...] = kpe
    kcat_ref[:, :kv_lora] = ckv.astype(BF16)
    kcat_ref[:, kv_lora:] = kpe.astype(BF16)
    ckvt_ref[...] = ckv.T.astype(BF16)

    q = _dot(c_q, w2_ref[...])
    n0 = heads * nope
    n1 = n0 + heads * rope
    reps = heads * rope // LANE
    cos_h = jnp.concatenate([cos] * reps, axis=-1)
    sin_h = jnp.concatenate([sin] * reps, axis=-1)
    qpe = (q[:, n0:n1] * cos_h + q[:, n1:] * sin_h) * scale
    for hh in range(heads):
        qn = q[:, hh * nope:(hh + 1) * nope].astype(BF16)
        qlat = _dot(qn, wukt_ref[hh]) * scale
        qcat_ref[hh, :, :kv_lora] = qlat.astype(BF16)
        qcat_ref[hh, :, kv_lora:] = qpe[:, hh * rope:(hh + 1) * rope].astype(BF16)


def _mla_project(x2d, cos, sin, gmix, w1, gq, gkv, w2, wukt, *, batch, seq, tm):
    d_model = x2d.shape[1]
    heads, nope, kv_lora = wukt.shape
    q_lora = gq.shape[1]
    rope = (w2.shape[1] - heads * nope) // (2 * heads)
    kdim = kv_lora + rope
    nt = seq // tm
    rows = batch * seq
    kern = functools.partial(_mla_proj_kernel, heads=heads, q_lora=q_lora, kv_lora=kv_lora,
                             nope=nope, rope=rope)
    return pl.pallas_call(
        kern,
        grid=(batch * nt,),
        in_specs=[
            pl.BlockSpec((tm, d_model), lambda i: (i, 0)),
            pl.BlockSpec((tm, LANE), lambda i: (i % nt, 0)),
            pl.BlockSpec((tm, LANE), lambda i: (i % nt, 0)),
            _const_spec(gmix.shape), _const_spec(w1.shape), _const_spec(gq.shape),
            _const_spec(gkv.shape), _const_spec(w2.shape), _const_spec(wukt.shape),
        ],
        out_specs=[
            pl.BlockSpec((None, heads, tm, kdim), lambda i: (i // nt, 0, i % nt, 0)),
            pl.BlockSpec((None, tm, kdim), lambda i: (i // nt, i % nt, 0)),
            pl.BlockSpec((None, None, kv_lora, tm), lambda i: (i // nt, i % nt, 0, 0)),
            pl.BlockSpec((tm, kv_lora), lambda i: (i, 0)),
            pl.BlockSpec((tm, rope), lambda i: (i, 0)),
        ],
        out_shape=[
            jax.ShapeDtypeStruct((batch, heads, seq, kdim), BF16),
            jax.ShapeDtypeStruct((batch, seq, kdim), BF16),
            jax.ShapeDtypeStruct((batch, nt, kv_lora, tm), BF16),
            jax.ShapeDtypeStruct((rows, kv_lora), F32),
            jax.ShapeDtypeStruct((rows, rope), F32),
        ],
        compiler_params=_params("arbitrary"),
        name="mla_project",
    )(x2d, cos, sin, gmix, w1, gq, gkv, w2, wukt)


def _mla_flash_kernel(q_ref, k_ref, vt_ref, o_ref, m_ref, l_ref, acc_ref, s_ref, *, heads, tq,
                      tk):
    qi = pl.program_id(1)
    lanes = heads * tq
    q = q_ref[...].reshape(lanes, q_ref.shape[-1])
    m_ref[...] = jnp.full(m_ref.shape, -jnp.inf, F32)
    l_ref[...] = jnp.zeros(l_ref.shape, F32)
    acc_ref[...] = jnp.zeros(acc_ref.shape, F32)

    def scores(j):
        return _dot_nt(k_ref[j], q)

    def update(j, s):
        m_prev = m_ref[...]
        m_new = jnp.maximum(m_prev, jnp.max(s, axis=0, keepdims=True))
        alpha = jnp.exp2(m_prev - m_new)
        p = jnp.exp2(s - m_new)
        l_ref[...] = alpha * l_ref[...] + jnp.sum(p, axis=0, keepdims=True)
        acc_ref[...] = alpha * acc_ref[...] + _dot(vt_ref[j], p.astype(BF16))
        m_ref[...] = m_new

    n_full = (qi * tq) // tk

    def masked(s):
        key = n_full * tk + lax.broadcasted_iota(jnp.int32, (tk, lanes), 0)
        tok = qi * tq + lax.rem(lax.broadcasted_iota(jnp.int32, (tk, lanes), 1), tq)
        return jnp.where(key <= tok, s, -jnp.inf)

    s_ref[...] = scores(0)

    def pair(jj, carry):
        j = 2 * jj
        s_b = scores(j + 1)
        update(j, s_ref[...])
        s_ref[...] = scores(j + 2)
        update(j + 1, s_b)
        return carry

    n_pairs = n_full // 2
    lax.fori_loop(0, n_pairs, pair, 0)

    @pl.when(n_full % 2 == 1)
    def _():
        s_d = scores(n_full)
        update(n_full - 1, s_ref[...])
        update(n_full, masked(s_d))

    @pl.when(n_full % 2 == 0)
    def _():
        update(n_full, masked(s_ref[...]))

    for hh in range(heads):
        cols = slice(hh * tq, (hh + 1) * tq)
        o_t = acc_ref[:, cols] / l_ref[:, cols]
        o_ref[hh] = o_t.T.astype(o_ref.dtype)


def _mla_flash(qcat, kcat, ckvt, *, tq, tk):
    batch, heads, seq, kdim = qcat.shape
    kv_lora = ckvt.shape[2]
    nk = seq // tk
    kern = functools.partial(_mla_flash_kernel, heads=heads, tq=tq, tk=tk)
    lanes = heads * tq
    return pl.pallas_call(
        kern,
        grid=(batch, seq // tq),
        in_specs=[
            pl.BlockSpec((None, heads, tq, kdim), lambda b, i: (b, 0, i, 0)),
            pl.BlockSpec((None, nk, tk, kdim), lambda b, i: (b, 0, 0, 0)),
            pl.BlockSpec((None, nk, kv_lora, tk), lambda b, i: (b, 0, 0, 0)),
        ],
        out_specs=pl.BlockSpec((None, heads, tq, kv_lora), lambda b, i: (b, 0, i, 0)),
        out_shape=jax.ShapeDtypeStruct((batch, heads, seq, kv_lora), BF16),
        scratch_shapes=[pltpu.VMEM((1, lanes), F32), pltpu.VMEM((1, lanes), F32),
                        pltpu.VMEM((kv_lora, lanes), F32), pltpu.VMEM((tk, lanes), F32)],
        compiler_params=_params("arbitrary", "arbitrary"),
        name="mla_flash",
    )(qcat, kcat.reshape(batch, nk, tk, kdim), ckvt)


def _mla_decode_kernel(pt_ref, q_ref, knew_ref, ckv_hbm, kpe_hbm, o_ref, cbuf, rbuf, sem,
                       s_ref, *, layer, n_pages, page, kv_lora, chunk_pages):
    b = pl.program_id(0)
    nb = pl.num_programs(0)
    slot = lax.rem(b, 2)

    def page_copies(seq, slot_, p):
        pg = pt_ref[seq * n_pages + p]
        c = p // chunk_pages
        return (pltpu.make_async_copy(ckv_hbm.at[layer, pg], cbuf.at[slot_, p],
                                      sem.at[0, slot_, c]),
                pltpu.make_async_copy(kpe_hbm.at[layer, pg], rbuf.at[slot_, p],
                                      sem.at[1, slot_, c]))

    @pl.when(b == 0)
    def _():
        def body(p, carry):
            for c in page_copies(0, 0, p):
                c.start()
            return carry
        lax.fori_loop(0, n_pages, body, 0)

    q = q_ref[0]
    q_lat = q[:, :kv_lora]
    q_pe = q[:, kv_lora:]
    ck = chunk_pages * page
    n_chunks = n_pages // chunk_pages

    def wait_chunk(c):
        for p in range(c * chunk_pages, (c + 1) * chunk_pages):
            for cp in page_copies(b, slot, p):
                cp.wait()

    def prefetch_chunk(c):
        @pl.when(b + 1 < nb)
        def _():
            for p in range(c * chunk_pages, (c + 1) * chunk_pages):
                for cp in page_copies(b + 1, 1 - slot, p):
                    cp.start()

    def ckv_chunk(c):
        lo = c * chunk_pages
        return cbuf[slot, lo:lo + chunk_pages].reshape(ck, kv_lora).astype(BF16)

    def kpe_t_chunk(c):
        lo = c * chunk_pages
        return jnp.concatenate([rbuf[slot, lo + i] for i in range(chunk_pages)],
                               axis=-1).astype(BF16)

    for c in range(n_chunks):
        wait_chunk(c)
        s_ref[:, c * ck:(c + 1) * ck] = _dot_nt(q_lat, ckv_chunk(c)) + _dot(q_pe, kpe_t_chunk(c))
        prefetch_chunk(c)

    knew = knew_ref[0]
    s_new = jnp.sum(q.astype(F32) * knew.astype(F32), axis=-1, keepdims=True)
    s_all = s_ref[...]
    m = jnp.maximum(jnp.max(s_all, axis=-1, keepdims=True), s_new)
    p_all = jnp.exp2(s_all - m)
    p_new = jnp.exp2(s_new - m)
    denom = jnp.sum(p_all, axis=-1, keepdims=True) + p_new
    s_ref[...] = p_all

    acc = p_new.astype(BF16).astype(F32) * knew[:, :kv_lora].astype(F32)
    for c in range(n_chunks):
        acc = acc + _dot(s_ref[:, c * ck:(c + 1) * ck].astype(BF16), ckv_chunk(c))
    o_ref[0] = (acc / denom).astype(o_ref.dtype)


def _mla_decode(page_table_flat, q_s, knew, cache_ckv, cache_kpe_t, *, layer, chunk_pages=16):
    n_seq, heads, kdim = q_s.shape
    _, _, page, kv_lora = cache_ckv.shape
    rope = cache_kpe_t.shape[2]
    n_pages = page_table_flat.shape[0] // n_seq
    chunk_pages = math.gcd(chunk_pages, n_pages)
    kern = functools.partial(_mla_decode_kernel, layer=layer, n_pages=n_pages, page=page,
                             kv_lora=kv_lora, chunk_pages=chunk_pages)
    grid_spec = pltpu.PrefetchScalarGridSpec(
        num_scalar_prefetch=1,
        grid=(n_seq,),
        in_specs=[
            pl.BlockSpec((1, heads, kdim), lambda b, pt: (b, 0, 0)),
            pl.BlockSpec((1, 1, kdim), lambda b, pt: (b, 0, 0)),
            pl.BlockSpec(memory_space=pl.ANY),
            pl.BlockSpec(memory_space=pl.ANY),
        ],
        out_specs=pl.BlockSpec((1, heads, kv_lora), lambda b, pt: (b, 0, 0)),
        scratch_shapes=[
            pltpu.VMEM((2, n_pages, page, kv_lora), F32),
            pltpu.VMEM((2, n_pages, rope, page), F32),
            pltpu.SemaphoreType.DMA((2, 2, n_pages // chunk_pages)),
            pltpu.VMEM((heads, n_pages * page), F32),
        ],
    )
    return pl.pallas_call(
        kern,
        grid_spec=grid_spec,
        out_shape=jax.ShapeDtypeStruct((n_seq, heads, kv_lora), BF16),
        compiler_params=_params("arbitrary"),
        name="mla_decode",
    )(page_table_flat, q_s, knew, cache_ckv, cache_kpe_t)


def _mla_out_kernel(o_ref, x_ref, wuv_ref, wo_ref, out_ref, *, heads):
    parts = [_dot(o_ref[hh], wuv_ref[hh]).astype(BF16) for hh in range(heads)]
    o = jnp.concatenate(parts, axis=-1)
    out_ref[...] = x_ref[...] + _dot(o, wo_ref[...])


def _mla_out(o_lat, x2d, wuv, wo, *, tm):
    batch, heads, seq, kv_lora = o_lat.shape
    d_model = x2d.shape[1]
    nt = seq // tm
    return pl.pallas_call(
        functools.partial(_mla_out_kernel, heads=heads),
        grid=(batch * nt,),
        in_specs=[
            pl.BlockSpec((None, heads, tm, kv_lora), lambda i: (i // nt, 0, i % nt, 0)),
            pl.BlockSpec((tm, d_model), lambda i: (i, 0)),
            _const_spec(wuv.shape), _const_spec(wo.shape),
        ],
        out_specs=pl.BlockSpec((tm, d_model), lambda i: (i, 0)),
        out_shape=jax.ShapeDtypeStruct(x2d.shape, F32),
        compiler_params=_params("arbitrary"),
        name="mla_out",
    )(o_lat, x2d, wuv, wo)


def _sb_proj_kernel(x_ref, gmix_ref, w_ref, q_ref, kbf_ref, vbf_ref, k_ref, v_ref, *,
                    heads, head_dim):
    h = _rmsnorm(x_ref[...], gmix_ref[...]).astype(BF16)
    y = _dot(h, w_ref[...])
    for hh in range(heads):
        q_ref[hh] = y[:, hh * head_dim:(hh + 1) * head_dim].astype(BF16)
    nq = heads * head_dim
    nk = k_ref.shape[-1]
    k = y[:, nq:nq + nk]
    v = y[:, nq + nk:]
    k_ref[...] = k
    v_ref[...] = v
    kbf_ref[...] = k.astype(BF16)
    vbf_ref[...] = v.astype(BF16)


def _sb_project(x2d, gmix, w, *, batch, seq, tm, heads, head_dim):
    d_model = x2d.shape[1]
    nk = (w.shape[1] - heads * head_dim) // 2
    nt = seq // tm
    rows = batch * seq
    return pl.pallas_call(
        functools.partial(_sb_proj_kernel, heads=heads, head_dim=head_dim),
        grid=(batch * nt,),
        in_specs=[pl.BlockSpec((tm, d_model), lambda i: (i, 0)),
                  _const_spec(gmix.shape), _const_spec(w.shape)],
        out_specs=[
            pl.BlockSpec((None, heads, tm, head_dim), lambda i: (i // nt, 0, i % nt, 0)),
            pl.BlockSpec((None, tm, nk), lambda i: (i // nt, i % nt, 0)),
            pl.BlockSpec((None, tm, nk), lambda i: (i // nt, i % nt, 0)),
            pl.BlockSpec((tm, nk), lambda i: (i, 0)),
            pl.BlockSpec((tm, nk), lambda i: (i, 0)),
        ],
        out_shape=[
            jax.ShapeDtypeStruct((batch, heads, seq, head_dim), BF16),
            jax.ShapeDtypeStruct((batch, seq, nk), BF16),
            jax.ShapeDtypeStruct((batch, seq, nk), BF16),
            jax.ShapeDtypeStruct((rows, nk), F32),
            jax.ShapeDtypeStruct((rows, nk), F32),
        ],
        compiler_params=_params("arbitrary"),
        name="sb_project",
    )(x2d, gmix, w)


def _sb_block(q, k, v, tri, carry, mask, inv_sqrt):
    z = _dot_nt(q, k) * inv_sqrt
    sp = jnp.log1p(jnp.exp(-jnp.abs(z)))
    log_beta = jnp.minimum(z, 0.0) - sp
    log_keep = jnp.minimum(-z, 0.0) - sp
    if mask is not None:
        log_keep = jnp.where(mask, log_keep, 0.0)
    hi = log_keep.astype(BF16)
    lo = (log_keep - hi.astype(F32)).astype(BF16)
    suffix = _dot(hi, tri) + _dot(lo, tri)
    a = jnp.exp(log_beta + suffix + carry)
    if mask is not None:
        a = jnp.where(mask, a, 0.0)
    return _dot(a.astype(BF16), v), jnp.sum(log_keep, axis=-1, keepdims=True)


def _sb_attn_kernel(q_ref, k_ref, v_ref, tri_ref, o_ref, carry_ref, acc_ref, *, group, tq,
                    head_dim):
    qi = pl.program_id(2)
    rows = group * tq
    inv_sqrt = 1.0 / math.sqrt(head_dim)
    q = q_ref[...].reshape(rows, head_dim)
    tri = tri_ref[...]

    row = lax.broadcasted_iota(jnp.int32, (tq, tq), 0)
    col = lax.broadcasted_iota(jnp.int32, (tq, tq), 1)
    mask = jnp.broadcast_to((col < row)[None], (group, tq, tq)).reshape(rows, tq)
    start = pl.multiple_of(qi * tq, tq)
    out, ksum = _sb_block(q, k_ref[pl.ds(start, tq), :], v_ref[pl.ds(start, tq), :], tri,
                          jnp.zeros((rows, 1), F32), mask, inv_sqrt)
    acc_ref[...] = out
    carry_ref[...] = ksum

    def cond(state):
        j, cmax = state
        return jnp.logical_and(j >= 0, cmax > SB_UNDERFLOW)

    def body(state):
        j, _ = state
        st = pl.multiple_of(j * tq, tq)
        carry = carry_ref[...]
        out, ksum = _sb_block(q, k_ref[pl.ds(st, tq), :], v_ref[pl.ds(st, tq), :], tri, carry,
                              None, inv_sqrt)
        acc_ref[...] += out
        carry = carry + ksum
        carry_ref[...] = carry
        return j - 1, jnp.max(carry)

    lax.while_loop(cond, body, (qi - 1, jnp.max(ksum)))
    o_ref[...] = acc_ref[...].reshape(group, tq, head_dim).astype(o_ref.dtype)


def _sb_attention(q, kbf, vbf, tri, *, tq):
    batch, heads, seq, head_dim = q.shape
    kv_heads = kbf.shape[-1] // head_dim
    group = heads // kv_heads
    rows = group * tq
    kern = functools.partial(_sb_attn_kernel, group=group, tq=tq, head_dim=head_dim)
    return pl.pallas_call(
        kern,
        grid=(batch, kv_heads, seq // tq),
        in_specs=[
            pl.BlockSpec((None, group, tq, head_dim), lambda b, h, i: (b, h, i, 0)),
            pl.BlockSpec((None, seq, head_dim), lambda b, h, i: (b, 0, h)),
            pl.BlockSpec((None, seq, head_dim), lambda b, h, i: (b, 0, h)),
            _const_spec(tri.shape),
        ],
        out_specs=pl.BlockSpec((None, group, tq, head_dim), lambda b, h, i: (b, h, i, 0)),
        out_shape=jax.ShapeDtypeStruct(q.shape, BF16),
        scratch_shapes=[pltpu.VMEM((rows, 1), F32), pltpu.VMEM((rows, head_dim), F32)],
        compiler_params=_params("arbitrary", "arbitrary", "arbitrary"),
        name="sb_attention",
    )(q, kbf, vbf, tri)


def _sb_decode_kernel(pt_ref, q_ref, tri_ref, k_hbm, v_hbm, o_ref, kbuf, vbuf, sem, *, layer,
                      n_pages, kv_heads, group, head_dim):
    b = pl.program_id(0)
    nb = pl.num_programs(0)
    bset = lax.rem(b, 2)
    heads = kv_heads * group
    inv_sqrt = 1.0 / math.sqrt(head_dim)
    newest = n_pages - 1

    def slot_of(p):
        return lax.rem(newest - p, 2)

    def copies(seq, set_, p):
        pg = pt_ref[seq * n_pages + p]
        slot_ = slot_of(p)
        return (pltpu.make_async_copy(k_hbm.at[layer, pg], kbuf.at[set_, slot_],
                                      sem.at[0, set_, slot_]),
                pltpu.make_async_copy(v_hbm.at[layer, pg], vbuf.at[set_, slot_],
                                      sem.at[1, set_, slot_]))

    def start_newest(seq, set_):
        for p in range(newest, max(newest - 2, -1), -1):
            for c in copies(seq, set_, p):
                c.start()

    @pl.when(b == 0)
    def _():
        start_newest(0, 0)

    @pl.when(b + 1 < nb)
    def _():
        start_newest(b + 1, 1 - bset)

    q = q_ref[0]
    tri = tri_ref[...]
    width = kbuf.shape[2]
    kv_of_row = lax.broadcasted_iota(jnp.int32, (heads, width), 0) // group
    kv_of_col = lax.rem(lax.broadcasted_iota(jnp.int32, (heads, width), 1), kv_heads)
    mask = kv_of_row == kv_of_col

    def cond(state):
        p, cmax, _, _ = state
        return jnp.logical_and(p >= 0, cmax > SB_UNDERFLOW)

    def body(state):
        p, _, carry, acc = state
        slot_ = slot_of(p)
        for c in copies(b, bset, p):
            c.wait()
        out, ksum = _sb_block(q, kbuf[bset, slot_].astype(BF16), vbuf[bset, slot_].astype(BF16),
                              tri, carry, mask, inv_sqrt)
        carry = carry + ksum
        cmax = jnp.max(carry)

        @pl.when(jnp.logical_and(p >= 2, cmax > SB_UNDERFLOW))
        def _():
            for c in copies(b, bset, p - 2):
                c.start()

        return p - 1, cmax, carry, acc + out

    init = (jnp.int32(newest), jnp.float32(0.0), jnp.zeros((heads, 1), F32),
            jnp.zeros((heads, head_dim), F32))
    p_end, _, _, acc = lax.while_loop(cond, body, init)

    @pl.when(p_end >= 0)
    def _():
        for c in copies(b, bset, p_end):
            c.wait()

    o_ref[0] = acc.astype(o_ref.dtype)


def _sb_decode(page_table_flat, q_s, tri, cache_k, cache_v, *, layer, kv_heads):
    n_seq, heads, head_dim = q_s.shape
    width = cache_k.shape[2]
    n_pages = page_table_flat.shape[0] // n_seq
    kern = functools.partial(_sb_decode_kernel, layer=layer, n_pages=n_pages, kv_heads=kv_heads,
                             group=heads // kv_heads, head_dim=head_dim)
    grid_spec = pltpu.PrefetchScalarGridSpec(
        num_scalar_prefetch=1,
        grid=(n_seq,),
        in_specs=[
            pl.BlockSpec((1, heads, head_dim), lambda b, pt: (b, 0, 0)),
            pl.BlockSpec(tri.shape, lambda b, pt: (0, 0)),
            pl.BlockSpec(memory_space=pl.ANY),
            pl.BlockSpec(memory_space=pl.ANY),
        ],
        out_specs=pl.BlockSpec((1, heads, head_dim), lambda b, pt: (b, 0, 0)),
        scratch_shapes=[
            pltpu.VMEM((2, 2, width, head_dim), F32),
            pltpu.VMEM((2, 2, width, head_dim), F32),
            pltpu.SemaphoreType.DMA((2, 2, 2)),
        ],
    )
    return pl.pallas_call(
        kern,
        grid_spec=grid_spec,
        out_shape=jax.ShapeDtypeStruct((n_seq, heads, head_dim), BF16),
        compiler_params=_params("arbitrary"),
        name="sb_decode",
    )(page_table_flat, q_s, tri, cache_k, cache_v)


def _sb_out_kernel(o_ref, x_ref, wo_ref, out_ref, *, heads):
    o = jnp.concatenate([o_ref[hh] for hh in range(heads)], axis=-1)
    out_ref[...] = x_ref[...] + _dot(o, wo_ref[...])


def _sb_out(o, x2d, wo, *, tm):
    batch, heads, seq, head_dim = o.shape
    d_model = x2d.shape[1]
    nt = seq // tm
    return pl.pallas_call(
        functools.partial(_sb_out_kernel, heads=heads),
        grid=(batch * nt,),
        in_specs=[
            pl.BlockSpec((None, heads, tm, head_dim), lambda i: (i // nt, 0, i % nt, 0)),
            pl.BlockSpec((tm, d_model), lambda i: (i, 0)),
            _const_spec(wo.shape),
        ],
        out_specs=pl.BlockSpec((tm, d_model), lambda i: (i, 0)),
        out_shape=jax.ShapeDtypeStruct(x2d.shape, F32),
        compiler_params=_params("arbitrary"),
        name="sb_out",
    )(o, x2d, wo)


def _ffn_gate(g2, g1, g, u, cw_ref, cb_ref, cols):
    w0 = cw_ref[0:1, cols]
    w1 = cw_ref[1:2, cols]
    w2 = cw_ref[2:3, cols]
    gc = cb_ref[:, cols] + ((g2 * w0 + g1 * w1) + g * w2)
    return (gc * jax.nn.sigmoid(gc) * u).astype(BF16)


def _ffn_prompt_kernel(x_ref, gn_ref, wup_ref, cw_ref, cb_ref, wdn_ref, out_ref, tail_ref,
                       act_ref, carry_ref, *, d_ff, tf, tiles_per_seq):
    i = pl.program_id(0)
    x = x_ref[...]
    tm = x.shape[0]
    h = _rmsnorm(x, gn_ref[...]).astype(BF16)

    @pl.when(i % tiles_per_seq == 0)
    def _():
        carry_ref[...] = jnp.zeros(carry_ref.shape, F32)

    row = lax.broadcasted_iota(jnp.int32, (SUBLANE, tf), 0)
    for c in range(d_ff // tf):
        cols = slice(c * tf, (c + 1) * tf)
        g = _dot(h, wup_ref[:, cols])
        u = _dot(h, wup_ref[:, d_ff + c * tf:d_ff + (c + 1) * tf])
        prev = carry_ref[:, cols]
        r1 = pltpu.roll(g, 1, 0)
        r2 = pltpu.roll(g, 2, 0)
        head1 = jnp.where(row < 1, pltpu.roll(prev, 1, 0), r1[:SUBLANE])
        head2 = jnp.where(row < 2, pltpu.roll(prev, 2, 0), r2[:SUBLANE])
        g1 = jnp.concatenate([head1, r1[SUBLANE:]], axis=0)
        g2 = jnp.concatenate([head2, r2[SUBLANE:]], axis=0)
        tail = g[tm - SUBLANE:, :]
        carry_ref[:, cols] = tail
        tail_ref[:, cols] = tail
        act_ref[:, cols] = _ffn_gate(g2, g1, g, u, cw_ref, cb_ref, cols)
    out_ref[...] = x + _dot(act_ref[...], wdn_ref[...])


def _ffn_prompt(x2d, gn, wup, cw, cb, wdn, *, batch, seq, tm, tf):
    d_model = x2d.shape[1]
    d_ff = wdn.shape[0]
    nt = seq // tm
    kern = functools.partial(_ffn_prompt_kernel, d_ff=d_ff, tf=tf, tiles_per_seq=nt)
    return pl.pallas_call(
        kern,
        grid=(batch * nt,),
        in_specs=[pl.BlockSpec((tm, d_model), lambda i: (i, 0)),
                  _const_spec(gn.shape), _const_spec(wup.shape), _const_spec(cw.shape),
                  _const_spec(cb.shape), _const_spec(wdn.shape)],
        out_specs=[pl.BlockSpec((tm, d_model), lambda i: (i, 0)),
                   pl.BlockSpec((None, SUBLANE, d_ff), lambda i: (i // nt, 0, 0))],
        out_shape=[jax.ShapeDtypeStruct(x2d.shape, F32),
                   jax.ShapeDtypeStruct((batch, SUBLANE, d_ff), F32)],
        scratch_shapes=[pltpu.VMEM((tm, d_ff), BF16), pltpu.VMEM((SUBLANE, d_ff), F32)],
        compiler_params=_params("arbitrary"),
        name="ffn_prompt",
    )(x2d, gn, wup, cw, cb, wdn)


def _ffn_sample_kernel(x_ref, s0_ref, s1_ref, gn_ref, wup_ref, cw_ref, cb_ref, wdn_ref,
                       out_ref, gate_ref, act_ref, *, d_ff, tf):
    x = x_ref[...]
    h = _rmsnorm(x, gn_ref[...]).astype(BF16)
    for c in range(d_ff // tf):
        cols = slice(c * tf, (c + 1) * tf)
        g = _dot(h, wup_ref[:, cols])
        u = _dot(h, wup_ref[:, d_ff + c * tf:d_ff + (c + 1) * tf])
        gate_ref[:, cols] = g
        act_ref[:, cols] = _ffn_gate(s0_ref[:, cols], s1_ref[:, cols], g, u, cw_ref, cb_ref, cols)
    out_ref[...] = x + _dot(act_ref[...], wdn_ref[...])


def _ffn_sample(x2d, s0, s1, gn, wup, cw, cb, wdn, *, tf):
    rows, d_model = x2d.shape
    d_ff = wdn.shape[0]
    kern = functools.partial(_ffn_sample_kernel, d_ff=d_ff, tf=tf)
    return pl.pallas_call(
        kern,
        grid=(1,),
        in_specs=[_const_spec(x2d.shape), _const_spec(s0.shape), _const_spec(s1.shape),
                  _const_spec(gn.shape), _const_spec(wup.shape), _const_spec(cw.shape),
                  _const_spec(cb.shape), _const_spec(wdn.shape)],
        out_specs=[pl.BlockSpec((rows, d_model), lambda i: (0, 0)),
                   pl.BlockSpec((rows, d_ff), lambda i: (0, 0))],
        out_shape=[jax.ShapeDtypeStruct(x2d.shape, F32),
                   jax.ShapeDtypeStruct((rows, d_ff), F32)],
        scratch_shapes=[pltpu.VMEM((rows, d_ff), BF16)],
        compiler_params=_params("arbitrary"),
        name="ffn_sample",
    )(x2d, s0, s1, gn, wup, cw, cb, wdn)


def _final_norm_kernel(x_ref, g_ref, o_ref):
    o_ref[...] = _rmsnorm(x_ref[...], g_ref[...])


def _final_norm(x2d, g, *, tm):
    rows, d_model = x2d.shape
    return pl.pallas_call(
        _final_norm_kernel,
        grid=(rows // tm,),
        in_specs=[pl.BlockSpec((tm, d_model), lambda i: (i, 0)), _const_spec(g.shape)],
        out_specs=pl.BlockSpec((tm, d_model), lambda i: (i, 0)),
        out_shape=jax.ShapeDtypeStruct(x2d.shape, F32),
        compiler_params=_params("arbitrary"),
        name="final_norm",
    )(x2d, g)


def _tiles(seq, d_ff):
    def fit(pref):
        t = min(pref, seq)
        while seq % t:
            t //= 2
        return t
    tf = 2 * LANE if d_ff % (2 * LANE) == 0 else LANE
    return dict(tm=fit(512), mla_tq=fit(256), sb_tq=fit(256), tf=tf)


def _rot_cols(w):
    half = w.shape[-1] // 2
    return jnp.concatenate([-w[..., half:], w[..., :half]], axis=-1)


def _rope_tables(pos, rope):
    half = rope // 2
    inv = ROPE_BASE ** (-jnp.arange(half, dtype=F32) / half)
    ang = pos.astype(F32)[:, None] * inv[None, :]
    reps = LANE // half
    return jnp.tile(jnp.cos(ang), (1, reps)), jnp.tile(jnp.sin(ang), (1, reps))


def kernel(x_prompt, x_sample, cache_mla_ckv, cache_mla_kpe, cache_sb_k, cache_sb_v, state_ffn_conv, page_table, norm_mix, norm_ffn, norm_final, mla_w_dq, mla_g_q, mla_w_uq, mla_w_dkv, mla_g_kv, mla_w_uk, mla_w_uv, mla_w_o, sb_w_qkv, sb_w_o, ffn_w_up, ffn_conv_w, ffn_conv_b, ffn_w_down):
    batch, seq, d_model = x_prompt.shape
    n_seq, dec_seq, _ = x_sample.shape
    assert dec_seq == 1, "the sample group decodes one token per sequence"
    depth = norm_mix.shape[0]
    d_ff = ffn_w_down.shape[1]
    n_pages, page = page_table.shape[1], cache_mla_ckv.shape[2]
    past = n_pages * page
    heads, nope = mla_w_uk.shape[2], mla_w_uk.shape[3]
    q_lora, kv_lora = mla_w_dq.shape[2], mla_w_uk.shape[1]
    rope = mla_w_uq.shape[3] - nope
    sb_kv, sb_dim = cache_sb_k.shape[3], cache_sb_k.shape[4]
    sb_heads = sb_w_o.shape[1] // sb_dim
    t = _tiles(seq, d_ff)

    cos_p, sin_p = _rope_tables(jnp.arange(seq, dtype=jnp.int32), rope)
    cos_s, sin_s = _rope_tables(past + jnp.arange(dec_seq, dtype=jnp.int32), rope)
    cos_s = jnp.broadcast_to(cos_s, (n_seq, LANE))
    sin_s = jnp.broadcast_to(sin_s, (n_seq, LANE))
    pt_flat = page_table.reshape(-1)
    cache_k2 = cache_sb_k.reshape(cache_sb_k.shape[:2] + (page * sb_kv, sb_dim))
    cache_v2 = cache_sb_v.reshape(cache_sb_v.shape[:2] + (page * sb_kv, sb_dim))
    cache_kpe_t = jnp.swapaxes(cache_mla_kpe, 2, 3)
    tri_p = jnp.tril(jnp.ones((t["sb_tq"], t["sb_tq"]), BF16), k=-1)
    tri_s = jnp.tril(jnp.ones((page * sb_kv, page * sb_kv), BF16), k=-1)

    xp = x_prompt.reshape(batch * seq, d_model)
    xs = x_sample.reshape(n_seq, d_model)
    row = lambda v: v.reshape(1, -1)
    outs = {k: [] for k in ("ckv_p", "kpe_p", "ckv_s", "kpe_s", "sbk_p", "sbv_p", "sbk_s",
                            "sbv_s", "conv_p", "conv_s")}

    for i in range(depth):
        j = i // 2
        gmix = row(norm_mix[i])
        if i % 2 == 0:
            pad = jnp.zeros((d_model, LANE - rope), F32)
            w_r = mla_w_dkv[j][:, kv_lora:]
            w1 = jnp.concatenate([mla_w_dq[j], mla_w_dkv[j][:, :kv_lora], w_r, pad,
                                  _rot_cols(w_r), pad], axis=1).astype(BF16)
            wq = mla_w_uq[j]
            w2 = jnp.concatenate([wq[:, :, :nope].reshape(q_lora, heads * nope),
                                  wq[:, :, nope:].reshape(q_lora, heads * rope),
                                  _rot_cols(wq[:, :, nope:]).reshape(q_lora, heads * rope)],
                                 axis=1).astype(BF16)
            wukt = jnp.transpose(mla_w_uk[j], (1, 2, 0)).astype(BF16)
            wuv = jnp.transpose(mla_w_uv[j], (1, 0, 2)).astype(BF16)
            wo = mla_w_o[j].astype(BF16)
            prm = (gmix, w1, row(mla_g_q[j]), row(mla_g_kv[j]), w2, wukt)

            qcat, kcat, ckvt, ckv_p, kpe_p = _mla_project(xp, cos_p, sin_p, *prm, batch=batch,
                                                          seq=seq, tm=t["tm"])
            o_p = _mla_flash(qcat, kcat, ckvt, tq=t["mla_tq"], tk=t["tm"])
            xp = _mla_out(o_p, xp, wuv, wo, tm=t["tm"])

            qcat_s, kcat_s, _, ckv_s, kpe_s = _mla_project(xs, cos_s, sin_s, *prm, batch=1,
                                                           seq=n_seq, tm=n_seq)
            q_s = jnp.transpose(qcat_s[0], (1, 0, 2))
            o_s = _mla_decode(pt_flat, q_s, kcat_s.reshape(n_seq, 1, -1), cache_mla_ckv,
                              cache_kpe_t, layer=j)
            xs = _mla_out(jnp.transpose(o_s, (1, 0, 2))[None], xs, wuv, wo, tm=n_seq)

            outs["ckv_p"].append(ckv_p.reshape(batch, seq, kv_lora))
            outs["kpe_p"].append(kpe_p.reshape(batch, seq, rope))
            outs["ckv_s"].append(ckv_s.reshape(n_seq, dec_seq, kv_lora))
            outs["kpe_s"].append(kpe_s.reshape(n_seq, dec_seq, rope))
        else:
            wqkv = sb_w_qkv[j].astype(BF16)
            wo = sb_w_o[j].astype(BF16)
            q_p, kbf, vbf, k_p, v_p = _sb_project(xp, gmix, wqkv, batch=batch, seq=seq,
                                                  tm=t["tm"], heads=sb_heads, head_dim=sb_dim)
            o_p = _sb_attention(q_p, kbf, vbf, tri_p, tq=t["sb_tq"])
            xp = _sb_out(o_p, xp, wo, tm=t["tm"])

            q_s, _, _, k_s, v_s = _sb_project(xs, gmix, wqkv, batch=1, seq=n_seq, tm=n_seq,
                                              heads=sb_heads, head_dim=sb_dim)
            o_s = _sb_decode(pt_flat, jnp.transpose(q_s[0], (1, 0, 2)), tri_s, cache_k2,
                             cache_v2, layer=j, kv_heads=sb_kv)
            xs = _sb_out(jnp.transpose(o_s, (1, 0, 2))[None], xs, wo, tm=n_seq)

            outs["sbk_p"].append(k_p.reshape(batch, seq, sb_kv, sb_dim))
            outs["sbv_p"].append(v_p.reshape(batch, seq, sb_kv, sb_dim))
            outs["sbk_s"].append(k_s.reshape(n_seq, dec_seq, sb_kv, sb_dim))
            outs["sbv_s"].append(v_s.reshape(n_seq, dec_seq, sb_kv, sb_dim))

        gn = row(norm_ffn[i])
        wup = ffn_w_up[i].astype(BF16)
        wdn = ffn_w_down[i].astype(BF16)
        cw, cb = ffn_conv_w[i], row(ffn_conv_b[i])
        xp, tail = _ffn_prompt(xp, gn, wup, cw, cb, wdn, batch=batch, seq=seq, tm=t["tm"],
                               tf=t["tf"])
        st = state_ffn_conv[i]
        xs, gate_s = _ffn_sample(xs, st[:, 0], st[:, 1], gn, wup, cw, cb, wdn, tf=t["tf"])
        outs["conv_p"].append(tail[:, SUBLANE - 2:])
        outs["conv_s"].append(jnp.stack([st[:, 1], gate_s], axis=1))

    gfin = row(norm_final)
    y_p = _final_norm(xp, gfin, tm=t["tm"]).reshape(batch, seq, d_model)
    y_s = _final_norm(xs, gfin, tm=n_seq).reshape(n_seq, dec_seq, d_model)
    st = lambda k: jnp.stack(outs[k])
    return (y_p, y_s, st("ckv_p"), st("kpe_p"), st("sbk_p"), st("sbv_p"), st("conv_p"),
            st("ckv_s"), st("kpe_s"), st("sbk_s"), st("sbv_s"), st("conv_s"))
```

```python
import functools
import math

import jax
import jax.numpy as jnp
from jax import lax
from jax.experimental import pallas as pl
from jax.experimental.pallas import tpu as pltpu

F32 = jnp.float32
BF16 = jnp.bfloat16
RMS_EPS = 1e-6
ROPE_BASE = 10000.0
LANE = 128
SUBLANE = 8
VMEM_LIMIT = 56 * 1024 * 1024
SB_UNDERFLOW = -150.0
_NT = (((1,), (1,)), ((), ()))


def _params(*semantics):
    return pltpu.CompilerParams(dimension_semantics=semantics, vmem_limit_bytes=VMEM_LIMIT)


def _const_spec(shape):
    nd = len(shape)
    return pl.BlockSpec(shape, lambda *_: (0,) * nd, pipeline_mode=pl.Buffered(1))


def _rmsnorm(x, g):
    return x * lax.rsqrt(jnp.mean(x * x, axis=-1, keepdims=True) + RMS_EPS) * g


def _dot(a, b):
    return jnp.dot(a, b, preferred_element_type=F32)


def _dot_nt(a, b):
    return lax.dot_general(a, b, _NT, preferred_element_type=F32)


def _mla_proj_kernel(x_ref, cos_ref, sin_ref, gmix_ref, w1_ref, gq_ref, gkv_ref, w2_ref,
                     wukt_ref, qcat_ref, kcat_ref, ckvt_ref, ckv_ref, kpe_ref, *, heads, q_lora,
                     kv_lora, nope, rope):
    scale = math.log2(math.e) / math.sqrt(nope + rope)
    h = _rmsnorm(x_ref[...], gmix_ref[...]).astype(BF16)
    y = _dot(h, w1_ref[...])
    c_q = _rmsnorm(y[:, :q_lora], gq_ref[...]).astype(BF16)
    ckv = _rmsnorm(y[:, q_lora:q_lora + kv_lora], gkv_ref[...])
    cos = cos_ref[...]
    sin = sin_ref[...]
    r0 = q_lora + kv_lora
    kpe = y[:, r0:r0 + rope] * cos[:, :rope] + y[:, r0 + LANE:r0 + LANE + rope] * sin[:, :rope]
    ckv_ref[...] = ckv
    kpe_ref[...] = kpe
    kcat_ref[:, :kv_lora] = ckv.astype(BF16)
    kcat_ref[:, kv_lora:] = kpe.astype(BF16)
    ckvt_ref[...] = ckv.T.astype(BF16)

    q = _dot(c_q, w2_ref[...])
    n0 = heads * nope
    n1 = n0 + heads * rope
    reps = heads * rope // LANE
    cos_h = jnp.concatenate([cos] * reps, axis=-1)
    sin_h = jnp.concatenate([sin] * reps, axis=-1)
    qpe = (q[:, n0:n1] * cos_h + q[:, n1:] * sin_h) * scale
    for hh in range(heads):
        qn = q[:, hh * nope:(hh + 1) * nope].astype(BF16)
        qlat = _dot(qn, wukt_ref[hh]) * scale
        qcat_ref[hh, :, :kv_lora] = qlat.astype(BF16)
        qcat_ref[hh, :, kv_lora:] = qpe[:, hh * rope:(hh + 1) * rope].astype(BF16)


def _mla_project(x2d, cos, sin, gmix, w1, gq, gkv, w2, wukt, *, batch, seq, tm):
    d_model = x2d.shape[1]
    heads, nope, kv_lora = wukt.shape
    q_lora = gq.shape[1]
    rope = (w2.shape[1] - heads * nope) // (2 * heads)
    kdim = kv_lora + rope
    nt = seq // tm
    rows = batch * seq
    kern = functools.partial(_mla_proj_kernel, heads=heads, q_lora=q_lora, kv_lora=kv_lora,
                             nope=nope, rope=rope)
    return pl.pallas_call(
        kern,
        grid=(batch * nt,),
        in_specs=[
            pl.BlockSpec((tm, d_model), lambda i: (i, 0)),
            pl.BlockSpec((tm, LANE), lambda i: (i % nt, 0)),
            pl.BlockSpec((tm, LANE), lambda i: (i % nt, 0)),
            _const_spec(gmix.shape), _const_spec(w1.shape), _const_spec(gq.shape),
            _const_spec(gkv.shape), _const_spec(w2.shape), _const_spec(wukt.shape),
        ],
        out_specs=[
            pl.BlockSpec((None, heads, tm, kdim), lambda i: (i // nt, 0, i % nt, 0)),
            pl.BlockSpec((None, tm, kdim), lambda i: (i // nt, i % nt, 0)),
            pl.BlockSpec((None, None, kv_lora, tm), lambda i: (i // nt, i % nt, 0, 0)),
            pl.BlockSpec((tm, kv_lora), lambda i: (i, 0)),
            pl.BlockSpec((tm, rope), lambda i: (i, 0)),
        ],
        out_shape=[
            jax.ShapeDtypeStruct((batch, heads, seq, kdim), BF16),
            jax.ShapeDtypeStruct((batch, seq, kdim), BF16),
            jax.ShapeDtypeStruct((batch, nt, kv_lora, tm), BF16),
            jax.ShapeDtypeStruct((rows, kv_lora), F32),
            jax.ShapeDtypeStruct((rows, rope), F32),
        ],
        compiler_params=_params("arbitrary"),
        name="mla_project",
    )(x2d, cos, sin, gmix, w1, gq, gkv, w2, wukt)


def _mla_flash_kernel(q_ref, k_ref, vt_ref, o_ref, m_ref, l_ref, acc_ref, s_ref, *, heads, tq,
                      tk):
    qi = pl.program_id(1)
    lanes = heads * tq
    q = q_ref[...].reshape(lanes, q_ref.shape[-1])
    m_ref[...] = jnp.full(m_ref.shape, -jnp.inf, F32)
    l_ref[...] = jnp.zeros(l_ref.shape, F32)
    acc_ref[...] = jnp.zeros(acc_ref.shape, F32)

    def scores(j):
        return _dot_nt(k_ref[j], q)

    def update(j, s):
        m_prev = m_ref[...]
        m_new = jnp.maximum(m_prev, jnp.max(s, axis=0, keepdims=True))
        alpha = jnp.exp2(m_prev - m_new)
        p = jnp.exp2(s - m_new)
        l_ref[...] = alpha * l_ref[...] + jnp.sum(p, axis=0, keepdims=True)
        acc_ref[...] = alpha * acc_ref[...] + _dot(vt_ref[j], p.astype(BF16))
        m_ref[...] = m_new

    n_full = (qi * tq) // tk

    def masked(s):
        key = n_full * tk + lax.broadcasted_iota(jnp.int32, (tk, lanes), 0)
        tok = qi * tq + lax.rem(lax.broadcasted_iota(jnp.int32, (tk, lanes), 1), tq)
        return jnp.where(key <= tok, s, -jnp.inf)

    s_ref[...] = scores(0)

    def pair(jj, carry):
        j = 2 * jj
        s_b = scores(j + 1)
        update(j, s_ref[...])
        s_ref[...] = scores(j + 2)
        update(j + 1, s_b)
        return carry

    n_pairs = n_full // 2
    lax.fori_loop(0, n_pairs, pair, 0)

    @pl.when(n_full % 2 == 1)
    def _():
        s_d = scores(n_full)
        update(n_full - 1, s_ref[...])
        update(n_full, masked(s_d))

    @pl.when(n_full % 2 == 0)
    def _():
        update(n_full, masked(s_ref[...]))

    for hh in range(heads):
        cols = slice(hh * tq, (hh + 1) * tq)
        o_t = acc_ref[:, cols] / l_ref[:, cols]
        o_ref[hh] = o_t.T.astype(o_ref.dtype)


def _mla_flash(qcat, kcat, ckvt, *, tq, tk):
    batch, heads, seq, kdim = qcat.shape
    kv_lora = ckvt.shape[2]
    nk = seq // tk
    kern = functools.partial(_mla_flash_kernel, heads=heads, tq=tq, tk=tk)
    lanes = heads * tq
    return pl.pallas_call(
        kern,
        grid=(batch, seq // tq),
        in_specs=[
            pl.BlockSpec((None, heads, tq, kdim), lambda b, i: (b, 0, i, 0)),
            pl.BlockSpec((None, nk, tk, kdim), lambda b, i: (b, 0, 0, 0)),
            pl.BlockSpec((None, nk, kv_lora, tk), lambda b, i: (b, 0, 0, 0)),
        ],
        out_specs=pl.BlockSpec((None, heads, tq, kv_lora), lambda b, i: (b, 0, i, 0)),
        out_shape=jax.ShapeDtypeStruct((batch, heads, seq, kv_lora), BF16),
        scratch_shapes=[pltpu.VMEM((1, lanes), F32), pltpu.VMEM((1, lanes), F32),
                        pltpu.VMEM((kv_lora, lanes), F32), pltpu.VMEM((tk, lanes), F32)],
        compiler_params=_params("arbitrary", "arbitrary"),
        name="mla_flash",
    )(qcat, kcat.reshape(batch, nk, tk, kdim), ckvt)


def _mla_decode_kernel(pt_ref, q_ref, knew_ref, ckv_hbm, kpe_hbm, o_ref, cbuf, rbuf, sem,
                       s_ref, *, layer, n_pages, page, kv_lora, chunk_pages):
    b = pl.program_id(0)
    nb = pl.num_programs(0)
    slot = lax.rem(b, 2)

    def page_copies(seq, slot_, p):
        pg = pt_ref[seq * n_pages + p]
        c = p // chunk_pages
        return (pltpu.make_async_copy(ckv_hbm.at[layer, pg], cbuf.at[slot_, p],
                                      sem.at[0, slot_, c]),
                pltpu.make_async_copy(kpe_hbm.at[layer, pg], rbuf.at[slot_, p],
                                      sem.at[1, slot_, c]))

    @pl.when(b == 0)
    def _():
        def body(p, carry):
            for c in page_copies(0, 0, p):
                c.start()
            return carry
        lax.fori_loop(0, n_pages, body, 0)

    q = q_ref[0]
    q_lat = q[:, :kv_lora]
    q_pe = q[:, kv_lora:]
    ck = chunk_pages * page
    n_chunks = n_pages // chunk_pages

    def wait_chunk(c):
        for p in range(c * chunk_pages, (c + 1) * chunk_pages):
            for cp in page_copies(b, slot, p):
                cp.wait()

    def prefetch_chunk(c):
        @pl.when(b + 1 < nb)
        def _():
            for p in range(c * chunk_pages, (c + 1) * chunk_pages):
                for cp in page_copies(b + 1, 1 - slot, p):
                    cp.start()

    def ckv_chunk(c):
        lo = c * chunk_pages
        return cbuf[slot, lo:lo + chunk_pages].reshape(ck, kv_lora).astype(BF16)

    def kpe_t_chunk(c):
        lo = c * chunk_pages
        return jnp.concatenate([rbuf[slot, lo + i] for i in range(chunk_pages)],
                               axis=-1).astype(BF16)

    for c in range(n_chunks):
        wait_chunk(c)
        s_ref[:, c * ck:(c + 1) * ck] = _dot_nt(q_lat, ckv_chunk(c)) + _dot(q_pe, kpe_t_chunk(c))
        prefetch_chunk(c)

    knew = knew_ref[0]
    s_new = jnp.sum(q.astype(F32) * knew.astype(F32), axis=-1, keepdims=True)
    s_all = s_ref[...]
    m = jnp.maximum(jnp.max(s_all, axis=-1, keepdims=True), s_new)
    p_all = jnp.exp2(s_all - m)
    p_new = jnp.exp2(s_new - m)
    denom = jnp.sum(p_all, axis=-1, keepdims=True) + p_new
    s_ref[...] = p_all

    acc = p_new.astype(BF16).astype(F32) * knew[:, :kv_lora].astype(F32)
    for c in range(n_chunks):
        acc = acc + _dot(s_ref[:, c * ck:(c + 1) * ck].astype(BF16), ckv_chunk(c))
    o_ref[0] = (acc / denom).astype(o_ref.dtype)


def _mla_decode(page_table_flat, q_s, knew, cache_ckv, cache_kpe_t, *, layer, chunk_pages=16):
    n_seq, heads, kdim = q_s.shape
    _, _, page, kv_lora = cache_ckv.shape
    rope = cache_kpe_t.shape[2]
    n_pages = page_table_flat.shape[0] // n_seq
    chunk_pages = math.gcd(chunk_pages, n_pages)
    kern = functools.partial(_mla_decode_kernel, layer=layer, n_pages=n_pages, page=page,
                             kv_lora=kv_lora, chunk_pages=chunk_pages)
    grid_spec = pltpu.PrefetchScalarGridSpec(
        num_scalar_prefetch=1,
        grid=(n_seq,),
        in_specs=[
            pl.BlockSpec((1, heads, kdim), lambda b, pt: (b, 0, 0)),
            pl.BlockSpec((1, 1, kdim), lambda b, pt: (b, 0, 0)),
            pl.BlockSpec(memory_space=pl.ANY),
            pl.BlockSpec(memory_space=pl.ANY),
        ],
        out_specs=pl.BlockSpec((1, heads, kv_lora), lambda b, pt: (b, 0, 0)),
        scratch_shapes=[
            pltpu.VMEM((2, n_pages, page, kv_lora), F32),
            pltpu.VMEM((2, n_pages, rope, page), F32),
            pltpu.SemaphoreType.DMA((2, 2, n_pages // chunk_pages)),
            pltpu.VMEM((heads, n_pages * page), F32),
        ],
    )
    return pl.pallas_call(
        kern,
        grid_spec=grid_spec,
        out_shape=jax.ShapeDtypeStruct((n_seq, heads, kv_lora), BF16),
        compiler_params=_params("arbitrary"),
        name="mla_decode",
    )(page_table_flat, q_s, knew, cache_ckv, cache_kpe_t)


def _mla_out_kernel(o_ref, x_ref, wuv_ref, wo_ref, out_ref, *, heads):
    parts = [_dot(o_ref[hh], wuv_ref[hh]).astype(BF16) for hh in range(heads)]
    o = jnp.concatenate(parts, axis=-1)
    out_ref[...] = x_ref[...] + _dot(o, wo_ref[...])


def _mla_out(o_lat, x2d, wuv, wo, *, tm):
    batch, heads, seq, kv_lora = o_lat.shape
    d_model = x2d.shape[1]
    nt = seq // tm
    return pl.pallas_call(
        functools.partial(_mla_out_kernel, heads=heads),
        grid=(batch * nt,),
        in_specs=[
            pl.BlockSpec((None, heads, tm, kv_lora), lambda i: (i // nt, 0, i % nt, 0)),
            pl.BlockSpec((tm, d_model), lambda i: (i, 0)),
            _const_spec(wuv.shape), _const_spec(wo.shape),
        ],
        out_specs=pl.BlockSpec((tm, d_model), lambda i: (i, 0)),
        out_shape=jax.ShapeDtypeStruct(x2d.shape, F32),
        compiler_params=_params("arbitrary"),
        name="mla_out",
    )(o_lat, x2d, wuv, wo)


def _sb_proj_kernel(x_ref, gmix_ref, w_ref, q_ref, kbf_ref, vbf_ref, k_ref, v_ref, *,
                    heads, head_dim):
    scale = math.log2(math.e) / math.sqrt(head_dim)
    h = _rmsnorm(x_ref[...], gmix_ref[...]).astype(BF16)
    y = _dot(h, w_ref[...])
    for hh in range(heads):
        q_ref[hh] = (y[:, hh * head_dim:(hh + 1) * head_dim] * scale).astype(BF16)
    nq = heads * head_dim
    nk = k_ref.shape[-1]
    k = y[:, nq:nq + nk]
    v = y[:, nq + nk:]
    k_ref[...] = k
    v_ref[...] = v
    kbf_ref[...] = k.astype(BF16)
    vbf_ref[...] = v.astype(BF16)


def _sb_project(x2d, gmix, w, *, batch, seq, tm, heads, head_dim):
    d_model = x2d.shape[1]
    nk = (w.shape[1] - heads * head_dim) // 2
    nt = seq // tm
    rows = batch * seq
    return pl.pallas_call(
        functools.partial(_sb_proj_kernel, heads=heads, head_dim=head_dim),
        grid=(batch * nt,),
        in_specs=[pl.BlockSpec((tm, d_model), lambda i: (i, 0)),
                  _const_spec(gmix.shape), _const_spec(w.shape)],
        out_specs=[
            pl.BlockSpec((None, heads, tm, head_dim), lambda i: (i // nt, 0, i % nt, 0)),
            pl.BlockSpec((None, tm, nk), lambda i: (i // nt, i % nt, 0)),
            pl.BlockSpec((None, tm, nk), lambda i: (i // nt, i % nt, 0)),
            pl.BlockSpec((tm, nk), lambda i: (i, 0)),
            pl.BlockSpec((tm, nk), lambda i: (i, 0)),
        ],
        out_shape=[
            jax.ShapeDtypeStruct((batch, heads, seq, head_dim), BF16),
            jax.ShapeDtypeStruct((batch, seq, nk), BF16),
            jax.ShapeDtypeStruct((batch, seq, nk), BF16),
            jax.ShapeDtypeStruct((rows, nk), F32),
            jax.ShapeDtypeStruct((rows, nk), F32),
        ],
        compiler_params=_params("arbitrary"),
        name="sb_project",
    )(x2d, gmix, w)


def _sb_block(q, k, v, tri, carry, mask):
    z = _dot_nt(q, k)
    sp = jnp.log(1.0 + jnp.exp2(-jnp.abs(z))) * math.log2(math.e)
    log_beta = jnp.minimum(z, 0.0) - sp
    log_keep = jnp.minimum(-z, 0.0) - sp
    if mask is not None:
        log_keep = jnp.where(mask, log_keep, 0.0)
    hi = log_keep.astype(BF16)
    lo = (log_keep - hi.astype(F32)).astype(BF16)
    suffix = _dot(hi, tri) + _dot(lo, tri)
    a = jnp.exp2(log_beta + suffix + carry)
    if mask is not None:
        a = jnp.where(mask, a, 0.0)
    return _dot(a.astype(BF16), v), jnp.sum(log_keep, axis=-1, keepdims=True)


def _sb_attn_kernel(q_ref, k_ref, v_ref, tri_ref, o_ref, carry_ref, acc_ref, *, group, tq,
                    head_dim):
    qi = pl.program_id(2)
    rows = group * tq
    q = q_ref[...].reshape(rows, head_dim)
    tri = tri_ref[...]

    row = lax.broadcasted_iota(jnp.int32, (tq, tq), 0)
    col = lax.broadcasted_iota(jnp.int32, (tq, tq), 1)
    mask = jnp.broadcast_to((col < row)[None], (group, tq, tq)).reshape(rows, tq)
    start = pl.multiple_of(qi * tq, tq)
    out, ksum = _sb_block(q, k_ref[pl.ds(start, tq), :], v_ref[pl.ds(start, tq), :], tri,
                          jnp.zeros((rows, 1), F32), mask)
    acc_ref[...] = out
    carry_ref[...] = ksum

    def cond(state):
        j, cmax = state
        return jnp.logical_and(j >= 0, cmax > SB_UNDERFLOW)

    def body(state):
        j, _ = state
        st = pl.multiple_of(j * tq, tq)
        carry = carry_ref[...]
        out, ksum = _sb_block(q, k_ref[pl.ds(st, tq), :], v_ref[pl.ds(st, tq), :], tri, carry,
                              None)
        acc_ref[...] += out
        carry = carry + ksum
        carry_ref[...] = carry
        return j - 1, jnp.max(carry)

    lax.while_loop(cond, body, (qi - 1, jnp.max(ksum)))
    o_ref[...] = acc_ref[...].reshape(group, tq, head_dim).astype(o_ref.dtype)


def _sb_attention(q, kbf, vbf, tri, *, tq):
    batch, heads, seq, head_dim = q.shape
    kv_heads = kbf.shape[-1] // head_dim
    group = heads // kv_heads
    rows = group * tq
    kern = functools.partial(_sb_attn_kernel, group=group, tq=tq, head_dim=head_dim)
    return pl.pallas_call(
        kern,
        grid=(batch, kv_heads, seq // tq),
        in_specs=[
            pl.BlockSpec((None, group, tq, head_dim), lambda b, h, i: (b, h, i, 0)),
            pl.BlockSpec((None, seq, head_dim), lambda b, h, i: (b, 0, h)),
            pl.BlockSpec((None, seq, head_dim), lambda b, h, i: (b, 0, h)),
            _const_spec(tri.shape),
        ],
        out_specs=pl.BlockSpec((None, group, tq, head_dim), lambda b, h, i: (b, h, i, 0)),
        out_shape=jax.ShapeDtypeStruct(q.shape, BF16),
        scratch_shapes=[pltpu.VMEM((rows, 1), F32), pltpu.VMEM((rows, head_dim), F32)],
        compiler_params=_params("arbitrary", "arbitrary", "arbitrary"),
        name="sb_attention",
    )(q, kbf, vbf, tri)


def _sb_decode_kernel(pt_ref, q_ref, tri_ref, k_hbm, v_hbm, o_ref, kbuf, vbuf, sem, *, layer,
                      n_pages, kv_heads, group, head_dim):
    b = pl.program_id(0)
    nb = pl.num_programs(0)
    bset = lax.rem(b, 2)
    heads = kv_heads * group
    newest = n_pages - 1

    def slot_of(p):
        return lax.rem(newest - p, 2)

    def copies(seq, set_, p):
        pg = pt_ref[seq * n_pages + p]
        slot_ = slot_of(p)
        return (pltpu.make_async_copy(k_hbm.at[layer, pg], kbuf.at[set_, slot_],
                                      sem.at[0, set_, slot_]),
                pltpu.make_async_copy(v_hbm.at[layer, pg], vbuf.at[set_, slot_],
                                      sem.at[1, set_, slot_]))

    def start_newest(seq, set_):
        for p in range(newest, max(newest - 2, -1), -1):
            for c in copies(seq, set_, p):
                c.start()

    @pl.when(b == 0)
    def _():
        start_newest(0, 0)

    @pl.when(b + 1 < nb)
    def _():
        start_newest(b + 1, 1 - bset)

    q = q_ref[0]
    tri = tri_ref[...]
    width = kbuf.shape[2]
    kv_of_row = lax.broadcasted_iota(jnp.int32, (heads, width), 0) // group
    kv_of_col = lax.rem(lax.broadcasted_iota(jnp.int32, (heads, width), 1), kv_heads)
    mask = kv_of_row == kv_of_col

    def cond(state):
        p, cmax, _, _ = state
        return jnp.logical_and(p >= 0, cmax > SB_UNDERFLOW)

    def body(state):
        p, _, carry, acc = state
        slot_ = slot_of(p)
        for c in copies(b, bset, p):
            c.wait()
        out, ksum = _sb_block(q, kbuf[bset, slot_].astype(BF16), vbuf[bset, slot_].astype(BF16),
                              tri, carry, mask)
        carry = carry + ksum
        cmax = jnp.max(carry)

        @pl.when(jnp.logical_and(p >= 2, cmax > SB_UNDERFLOW))
        def _():
            for c in copies(b, bset, p - 2):
                c.start()

        return p - 1, cmax, carry, acc + out

    init = (jnp.int32(newest), jnp.float32(0.0), jnp.zeros((heads, 1), F32),
            jnp.zeros((heads, head_dim), F32))
    p_end, _, _, acc = lax.while_loop(cond, body, init)

    @pl.when(p_end >= 0)
    def _():
        for c in copies(b, bset, p_end):
            c.wait()

    o_ref[0] = acc.astype(o_ref.dtype)


def _sb_decode(page_table_flat, q_s, tri, cache_k, cache_v, *, layer, kv_heads):
    n_seq, heads, head_dim = q_s.shape
    width = cache_k.shape[2]
    n_pages = page_table_flat.shape[0] // n_seq
    kern = functools.partial(_sb_decode_kernel, layer=layer, n_pages=n_pages, kv_heads=kv_heads,
                             group=heads // kv_heads, head_dim=head_dim)
    grid_spec = pltpu.PrefetchScalarGridSpec(
        num_scalar_prefetch=1,
        grid=(n_seq,),
        in_specs=[
            pl.BlockSpec((1, heads, head_dim), lambda b, pt: (b, 0, 0)),
            pl.BlockSpec(tri.shape, lambda b, pt: (0, 0)),
            pl.BlockSpec(memory_space=pl.ANY),
            pl.BlockSpec(memory_space=pl.ANY),
        ],
        out_specs=pl.BlockSpec((1, heads, head_dim), lambda b, pt: (b, 0, 0)),
        scratch_shapes=[
            pltpu.VMEM((2, 2, width, head_dim), F32),
            pltpu.VMEM((2, 2, width, head_dim), F32),
            pltpu.SemaphoreType.DMA((2, 2, 2)),
        ],
    )
    return pl.pallas_call(
        kern,
        grid_spec=grid_spec,
        out_shape=jax.ShapeDtypeStruct((n_seq, heads, head_dim), BF16),
        compiler_params=_params("arbitrary"),
        name="sb_decode",
    )(page_table_flat, q_s, tri, cache_k, cache_v)


def _sb_out_kernel(o_ref, x_ref, wo_ref, out_ref, *, heads):
    o = jnp.concatenate([o_ref[hh] for hh in range(heads)], axis=-1)
    out_ref[...] = x_ref[...] + _dot(o, wo_ref[...])


def _sb_out(o, x2d, wo, *, tm):
    batch, heads, seq, head_dim = o.shape
    d_model = x2d.shape[1]
    nt = seq // tm
    return pl.pallas_call(
        functools.partial(_sb_out_kernel, heads=heads),
        grid=(batch * nt,),
        in_specs=[
            pl.BlockSpec((None, heads, tm, head_dim), lambda i: (i // nt, 0, i % nt, 0)),
            pl.BlockSpec((tm, d_model), lambda i: (i, 0)),
            _const_spec(wo.shape),
        ],
        out_specs=pl.BlockSpec((tm, d_model), lambda i: (i, 0)),
        out_shape=jax.ShapeDtypeStruct(x2d.shape, F32),
        compiler_params=_params("arbitrary"),
        name="sb_out",
    )(o, x2d, wo)


def _ffn_gate(g2, g1, g, u, cw_ref, cb_ref, cols):
    w0 = cw_ref[0:1, cols]
    w1 = cw_ref[1:2, cols]
    w2 = cw_ref[2:3, cols]
    gc = cb_ref[:, cols] + ((g2 * w0 + g1 * w1) + g * w2)
    return (gc * jax.nn.sigmoid(gc) * u).astype(BF16)


def _layer_spec(shape, layer):
    nd = len(shape)
    return pl.BlockSpec((None,) + tuple(shape[1:]), lambda *_: (layer,) + (0,) * (nd - 1),
                        pipeline_mode=pl.Buffered(1))


def _ffn_finish(res, gfin_ref, final):
    return _rmsnorm(res, gfin_ref[...]) if final else res


def _ffn_prompt_kernel(x_ref, gn_ref, wup_ref, cw_ref, cb_ref, wdn_ref, gfin_ref, out_ref,
                       tail_ref, act_ref, carry_ref, *, d_ff, tf, tiles_per_seq, final):
    i = pl.program_id(0)
    x = x_ref[...]
    tm = x.shape[0]
    h = _rmsnorm(x, gn_ref[...]).astype(BF16)

    @pl.when(i % tiles_per_seq == 0)
    def _():
        carry_ref[...] = jnp.zeros(carry_ref.shape, F32)

    row = lax.broadcasted_iota(jnp.int32, (SUBLANE, tf), 0)
    for c in range(d_ff // tf):
        cols = slice(c * tf, (c + 1) * tf)
        g = _dot(h, wup_ref[:, cols])
        u = _dot(h, wup_ref[:, d_ff + c * tf:d_ff + (c + 1) * tf])
        prev = carry_ref[:, cols]
        r1 = pltpu.roll(g, 1, 0)
        r2 = pltpu.roll(g, 2, 0)
        head1 = jnp.where(row < 1, pltpu.roll(prev, 1, 0), r1[:SUBLANE])
        head2 = jnp.where(row < 2, pltpu.roll(prev, 2, 0), r2[:SUBLANE])
        g1 = jnp.concatenate([head1, r1[SUBLANE:]], axis=0)
        g2 = jnp.concatenate([head2, r2[SUBLANE:]], axis=0)
        tail = g[tm - SUBLANE:, :]
        carry_ref[:, cols] = tail
        tail_ref[:, cols] = tail
        act_ref[:, cols] = _ffn_gate(g2, g1, g, u, cw_ref, cb_ref, cols)
    out_ref[...] = _ffn_finish(x + _dot(act_ref[...], wdn_ref[...]), gfin_ref, final)


def _ffn_prompt(x2d, gn, wup, cw, cb, wdn, gfin, *, layer, final, batch, seq, tm, tf):
    d_model = x2d.shape[1]
    d_ff = wdn.shape[1]
    nt = seq // tm
    kern = functools.partial(_ffn_prompt_kernel, d_ff=d_ff, tf=tf, tiles_per_seq=nt,
                             final=final)
    return pl.pallas_call(
        kern,
        grid=(batch * nt,),
        in_specs=[pl.BlockSpec((tm, d_model), lambda i: (i, 0)),
                  _const_spec(gn.shape), _layer_spec(wup.shape, layer), _const_spec(cw.shape),
                  _const_spec(cb.shape), _layer_spec(wdn.shape, layer),
                  _const_spec(gfin.shape)],
        out_specs=[pl.BlockSpec((tm, d_model), lambda i: (i, 0)),
                   pl.BlockSpec((None, SUBLANE, d_ff), lambda i: (i // nt, 0, 0))],
        out_shape=[jax.ShapeDtypeStruct(x2d.shape, F32),
                   jax.ShapeDtypeStruct((batch, SUBLANE, d_ff), F32)],
        scratch_shapes=[pltpu.VMEM((tm, d_ff), BF16), pltpu.VMEM((SUBLANE, d_ff), F32)],
        compiler_params=_params("arbitrary"),
        name="ffn_prompt",
    )(x2d, gn, wup, cw, cb, wdn, gfin)


def _ffn_sample_kernel(x_ref, s0_ref, s1_ref, gn_ref, wup_ref, cw_ref, cb_ref, wdn_ref,
                       gfin_ref, out_ref, gate_ref, act_ref, *, d_ff, tf, final):
    x = x_ref[...]
    h = _rmsnorm(x, gn_ref[...]).astype(BF16)
    for c in range(d_ff // tf):
        cols = slice(c * tf, (c + 1) * tf)
        g = _dot(h, wup_ref[:, cols])
        u = _dot(h, wup_ref[:, d_ff + c * tf:d_ff + (c + 1) * tf])
        gate_ref[:, cols] = g
        act_ref[:, cols] = _ffn_gate(s0_ref[:, cols], s1_ref[:, cols], g, u, cw_ref, cb_ref, cols)
    out_ref[...] = _ffn_finish(x + _dot(act_ref[...], wdn_ref[...]), gfin_ref, final)


def _ffn_sample(x2d, s0, s1, gn, wup, cw, cb, wdn, gfin, *, layer, final, tf):
    rows, d_model = x2d.shape
    d_ff = wdn.shape[1]
    kern = functools.partial(_ffn_sample_kernel, d_ff=d_ff, tf=tf, final=final)
    return pl.pallas_call(
        kern,
        grid=(1,),
        in_specs=[_const_spec(x2d.shape), _const_spec(s0.shape), _const_spec(s1.shape),
                  _const_spec(gn.shape), _layer_spec(wup.shape, layer), _const_spec(cw.shape),
                  _const_spec(cb.shape), _layer_spec(wdn.shape, layer),
                  _const_spec(gfin.shape)],
        out_specs=[pl.BlockSpec((rows, d_model), lambda i: (0, 0)),
                   pl.BlockSpec((rows, d_ff), lambda i: (0, 0))],
        out_shape=[jax.ShapeDtypeStruct(x2d.shape, F32),
                   jax.ShapeDtypeStruct((rows, d_ff), F32)],
        scratch_shapes=[pltpu.VMEM((rows, d_ff), BF16)],
        compiler_params=_params("arbitrary"),
        name="ffn_sample",
    )(x2d, s0, s1, gn, wup, cw, cb, wdn, gfin)


def _tiles(seq, d_ff):
    def fit(pref):
        t = min(pref, seq)
        while seq % t:
            t //= 2
        return t
    tf = 2 * LANE if d_ff % (2 * LANE) == 0 else LANE
    return dict(tm=fit(512), mla_tq=fit(256), sb_tq=fit(256), tf=tf)


def _rot_cols(w):
    half = w.shape[-1] // 2
    return jnp.concatenate([-w[..., half:], w[..., :half]], axis=-1)


def _rope_tables(pos, rope):
    half = rope // 2
    inv = ROPE_BASE ** (-jnp.arange(half, dtype=F32) / half)
    ang = pos.astype(F32)[:, None] * inv[None, :]
    reps = LANE // half
    return jnp.tile(jnp.cos(ang), (1, reps)), jnp.tile(jnp.sin(ang), (1, reps))


def kernel(x_prompt, x_sample, cache_mla_ckv, cache_mla_kpe, cache_sb_k, cache_sb_v, state_ffn_conv, page_table, norm_mix, norm_ffn, norm_final, mla_w_dq, mla_g_q, mla_w_uq, mla_w_dkv, mla_g_kv, mla_w_uk, mla_w_uv, mla_w_o, sb_w_qkv, sb_w_o, ffn_w_up, ffn_conv_w, ffn_conv_b, ffn_w_down):
    batch, seq, d_model = x_prompt.shape
    n_seq, dec_seq, _ = x_sample.shape
    assert dec_seq == 1, "the sample group decodes one token per sequence"
    depth = norm_mix.shape[0]
    d_ff = ffn_w_down.shape[1]
    n_pages, page = page_table.shape[1], cache_mla_ckv.shape[2]
    past = n_pages * page
    heads, nope = mla_w_uk.shape[2], mla_w_uk.shape[3]
    q_lora, kv_lora = mla_w_dq.shape[2], mla_w_uk.shape[1]
    rope = mla_w_uq.shape[3] - nope
    sb_kv, sb_dim = cache_sb_k.shape[3], cache_sb_k.shape[4]
    sb_heads = sb_w_o.shape[1] // sb_dim
    t = _tiles(seq, d_ff)

    cos_p, sin_p = _rope_tables(jnp.arange(seq, dtype=jnp.int32), rope)
    cos_s, sin_s = _rope_tables(past + jnp.arange(dec_seq, dtype=jnp.int32), rope)
    cos_s = jnp.broadcast_to(cos_s, (n_seq, LANE))
    sin_s = jnp.broadcast_to(sin_s, (n_seq, LANE))
    pt_flat = page_table.reshape(-1)
    cache_k2 = cache_sb_k.reshape(cache_sb_k.shape[:2] + (page * sb_kv, sb_dim))
    cache_v2 = cache_sb_v.reshape(cache_sb_v.shape[:2] + (page * sb_kv, sb_dim))
    cache_kpe_t = jnp.swapaxes(cache_mla_kpe, 2, 3)
    tri_p = jnp.tril(jnp.ones((t["sb_tq"], t["sb_tq"]), BF16), k=-1)
    tri_s = jnp.tril(jnp.ones((page * sb_kv, page * sb_kv), BF16), k=-1)

    xp = x_prompt.reshape(batch * seq, d_model)
    xs = x_sample.reshape(n_seq, d_model)
    row = lambda v: v.reshape(1, -1)
    wup_all = ffn_w_up.astype(BF16)
    wdn_all = ffn_w_down.astype(BF16)
    gfin = row(norm_final)
    outs = {k: [] for k in ("ckv_p", "kpe_p", "ckv_s", "kpe_s", "sbk_p", "sbv_p", "sbk_s",
                            "sbv_s", "conv_p", "conv_s")}

    for i in range(depth):
        j = i // 2
        gmix = row(norm_mix[i])
        if i % 2 == 0:
            pad = jnp.zeros((d_model, LANE - rope), F32)
            w_r = mla_w_dkv[j][:, kv_lora:]
            w1 = jnp.concatenate([mla_w_dq[j], mla_w_dkv[j][:, :kv_lora], w_r, pad,
                                  _rot_cols(w_r), pad], axis=1).astype(BF16)
            wq = mla_w_uq[j]
            w2 = jnp.concatenate([wq[:, :, :nope].reshape(q_lora, heads * nope),
                                  wq[:, :, nope:].reshape(q_lora, heads * rope),
                                  _rot_cols(wq[:, :, nope:]).reshape(q_lora, heads * rope)],
                                 axis=1).astype(BF16)
            wukt = jnp.transpose(mla_w_uk[j], (1, 2, 0)).astype(BF16)
            wuv = jnp.transpose(mla_w_uv[j], (1, 0, 2)).astype(BF16)
            wo = mla_w_o[j].astype(BF16)
            prm = (gmix, w1, row(mla_g_q[j]), row(mla_g_kv[j]), w2, wukt)

            qcat, kcat, ckvt, ckv_p, kpe_p = _mla_project(xp, cos_p, sin_p, *prm, batch=batch,
                                                          seq=seq, tm=t["tm"])
            o_p = _mla_flash(qcat, kcat, ckvt, tq=t["mla_tq"], tk=t["tm"])
            xp = _mla_out(o_p, xp, wuv, wo, tm=t["tm"])

            qcat_s, kcat_s, _, ckv_s, kpe_s = _mla_project(xs, cos_s, sin_s, *prm, batch=1,
                                                           seq=n_seq, tm=n_seq)
            q_s = jnp.transpose(qcat_s[0], (1, 0, 2))
            o_s = _mla_decode(pt_flat, q_s, kcat_s.reshape(n_seq, 1, -1), cache_mla_ckv,
                              cache_kpe_t, layer=j)
            xs = _mla_out(jnp.transpose(o_s, (1, 0, 2))[None], xs, wuv, wo, tm=n_seq)

            outs["ckv_p"].append(ckv_p.reshape(batch, seq, kv_lora))
            outs["kpe_p"].append(kpe_p.reshape(batch, seq, rope))
            outs["ckv_s"].append(ckv_s.reshape(n_seq, dec_seq, kv_lora))
            outs["kpe_s"].append(kpe_s.reshape(n_seq, dec_seq, rope))
        else:
            wqkv = sb_w_qkv[j].astype(BF16)
            wo = sb_w_o[j].astype(BF16)
            q_p, kbf, vbf, k_p, v_p = _sb_project(xp, gmix, wqkv, batch=batch, seq=seq,
                                                  tm=t["tm"], heads=sb_heads, head_dim=sb_dim)
            o_p = _sb_attention(q_p, kbf, vbf, tri_p, tq=t["sb_tq"])
            xp = _sb_out(o_p, xp, wo, tm=t["tm"])

            q_s, _, _, k_s, v_s = _sb_project(xs, gmix, wqkv, batch=1, seq=n_seq, tm=n_seq,
                                              heads=sb_heads, head_dim=sb_dim)
            o_s = _sb_decode(pt_flat, jnp.transpose(q_s[0], (1, 0, 2)), tri_s, cache_k2,
                             cache_v2, layer=j, kv_heads=sb_kv)
            xs = _sb_out(jnp.transpose(o_s, (1, 0, 2))[None], xs, wo, tm=n_seq)

            outs["sbk_p"].append(k_p.reshape(batch, seq, sb_kv, sb_dim))
            outs["sbv_p"].append(v_p.reshape(batch, seq, sb_kv, sb_dim))
            outs["sbk_s"].append(k_s.reshape(n_seq, dec_seq, sb_kv, sb_dim))
            outs["sbv_s"].append(v_s.reshape(n_seq, dec_seq, sb_kv, sb_dim))

        gn = row(norm_ffn[i])
        cw, cb = ffn_conv_w[i], row(ffn_conv_b[i])
        ffn = dict(layer=i, final=(i == depth - 1), tf=t["tf"])
        xp, tail = _ffn_prompt(xp, gn, wup_all, cw, cb, wdn_all, gfin, batch=batch, seq=seq,
                               tm=t["tm"], **ffn)
        st = state_ffn_conv[i]
        xs, gate_s = _ffn_sample(xs, st[:, 0], st[:, 1], gn, wup_all, cw, cb, wdn_all, gfin,
                                 **ffn)
        outs["conv_p"].append(tail[:, SUBLANE - 2:])
        outs["conv_s"].append(jnp.stack([st[:, 1], gate_s], axis=1))

    y_p = xp.reshape(batch, seq, d_model)
    y_s = xs.reshape(n_seq, dec_seq, d_model)
    st = lambda k: jnp.stack(outs[k])
    return (y_p, y_s, st("ckv_p"), st("kpe_p"), st("sbk_p"), st("sbv_p"), st("conv_p"),
            st("ckv_s"), st("kpe_s"), st("sbk_s"), st("sbv_s"), st("conv_s"))
```

```python
import functools
import math

import jax
import jax.numpy as jnp
from jax import lax
from jax.experimental import pallas as pl
from jax.experimental.pallas import tpu as pltpu

F32 = jnp.float32
BF16 = jnp.bfloat16
RMS_EPS = 1e-6
ROPE_BASE = 10000.0
LANE = 128
SUBLANE = 8
VMEM_LIMIT = 56 * 1024 * 1024
SB_UNDERFLOW = -150.0
_NT = (((1,), (1,)), ((), ()))


def _params(*semantics):
    return pltpu.CompilerParams(dimension_semantics=semantics, vmem_limit_bytes=VMEM_LIMIT)


def _const_spec(shape):
    nd = len(shape)
    return pl.BlockSpec(shape, lambda *_: (0,) * nd, pipeline_mode=pl.Buffered(1))


def _rmsnorm(x, g):
    return x * lax.rsqrt(jnp.mean(x * x, axis=-1, keepdims=True) + RMS_EPS) * g


def _dot(a, b):
    return jnp.dot(a, b, preferred_element_type=F32)


def _dot_nt(a, b):
    return lax.dot_general(a, b, _NT, preferred_element_type=F32)


def _mla_proj_kernel(x_ref, cos_ref, sin_ref, gmix_ref, w1_ref, gq_ref, gkv_ref, w2_ref,
                     wukt_ref, qcat_ref, kcat_ref, ckvt_ref, ckv_ref, kpe_ref, *, heads, q_lora,
                     kv_lora, nope, rope):
    scale = math.log2(math.e) / math.sqrt(nope + rope)
    h = _rmsnorm(x_ref[...], gmix_ref[...]).astype(BF16)
    y = _dot(h, w1_ref[...])
    c_q = _rmsnorm(y[:, :q_lora], gq_ref[...]).astype(BF16)
    ckv = _rmsnorm(y[:, q_lora:q_lora + kv_lora], gkv_ref[...])
    cos = cos_ref[...]
    sin = sin_ref[...]
    r0 = q_lora + kv_lora
    kpe = y[:, r0:r0 + rope] * cos[:, :rope] + y[:, r0 + LANE:r0 + LANE + rope] * sin[:, :rope]
    ckv_ref[...] = ckv
    kpe_ref[...] = kpe
    kcat_ref[:, :kv_lora] = ckv.astype(BF16)
    kcat_ref[:, kv_lora:] = kpe.astype(BF16)
    ckvt_ref[...] = ckv.T.astype(BF16)

    q = _dot(c_q, w2_ref[...])
    n0 = heads * nope
    n1 = n0 + heads * rope
    reps = heads * rope // LANE
    cos_h = jnp.concatenate([cos] * reps, axis=-1)
    sin_h = jnp.concatenate([sin] * reps, axis=-1)
    qpe = (q[:, n0:n1] * cos_h + q[:, n1:] * sin_h) * scale
    for hh in range(heads):
        qn = q[:, hh * nope:(hh + 1) * nope].astype(BF16)
        qlat = _dot(qn, wukt_ref[hh]) * scale
        qcat_ref[hh, :, :kv_lora] = qlat.astype(BF16)
        qcat_ref[hh, :, kv_lora:] = qpe[:, hh * rope:(hh + 1) * rope].astype(BF16)


def _mla_project(x2d, cos, sin, gmix, w1, gq, gkv, w2, wukt, *, batch, seq, tm):
    d_model = x2d.shape[1]
    heads, nope, kv_lora = wukt.shape
    q_lora = gq.shape[1]
    rope = (w2.shape[1] - heads * nope) // (2 * heads)
    kdim = kv_lora + rope
    nt = seq // tm
    rows = batch * seq
    kern = functools.partial(_mla_proj_kernel, heads=heads, q_lora=q_lora, kv_lora=kv_lora,
                             nope=nope, rope=rope)
    return pl.pallas_call(
        kern,
        grid=(batch * nt,),
        in_specs=[
            pl.BlockSpec((tm, d_model), lambda i: (i, 0)),
            pl.BlockSpec((tm, LANE), lambda i: (i % nt, 0)),
            pl.BlockSpec((tm, LANE), lambda i: (i % nt, 0)),
            _const_spec(gmix.shape), _const_spec(w1.shape), _const_spec(gq.shape),
            _const_spec(gkv.shape), _const_spec(w2.shape), _const_spec(wukt.shape),
        ],
        out_specs=[
            pl.BlockSpec((None, heads, tm, kdim), lambda i: (i // nt, 0, i % nt, 0)),
            pl.BlockSpec((None, tm, kdim), lambda i: (i // nt, i % nt, 0)),
            pl.BlockSpec((None, None, kv_lora, tm), lambda i: (i // nt, i % nt, 0, 0)),
            pl.BlockSpec((tm, kv_lora), lambda i: (i, 0)),
            pl.BlockSpec((tm, rope), lambda i: (i, 0)),
        ],
        out_shape=[
            jax.ShapeDtypeStruct((batch, heads, seq, kdim), BF16),
            jax.ShapeDtypeStruct((batch, seq, kdim), BF16),
            jax.ShapeDtypeStruct((batch, nt, kv_lora, tm), BF16),
            jax.ShapeDtypeStruct((rows, kv_lora), F32),
            jax.ShapeDtypeStruct((rows, rope), F32),
        ],
        compiler_params=_params("arbitrary"),
        name="mla_project",
    )(x2d, cos, sin, gmix, w1, gq, gkv, w2, wukt)


def _mla_flash_kernel(q_ref, k_ref, vt_ref, o_ref, m_ref, l_ref, acc_ref, s_ref, *, heads, tq,
                      tk):
    qi = pl.program_id(1)
    lanes = heads * tq
    q = q_ref[...].reshape(lanes, q_ref.shape[-1])
    m_ref[...] = jnp.full(m_ref.shape, -jnp.inf, F32)
    l_ref[...] = jnp.zeros(l_ref.shape, F32)
    acc_ref[...] = jnp.zeros(acc_ref.shape, F32)

    def scores(j):
        return _dot_nt(k_ref[j], q)

    def update(j, s):
        m_prev = m_ref[...]
        m_new = jnp.maximum(m_prev, jnp.max(s, axis=0, keepdims=True))
        alpha = jnp.exp2(m_prev - m_new)
        p = jnp.exp2(s - m_new)
        l_ref[...] = alpha * l_ref[...] + jnp.sum(p, axis=0, keepdims=True)
        acc_ref[...] = alpha * acc_ref[...] + _dot(vt_ref[j], p.astype(BF16))
        m_ref[...] = m_new

    n_full = (qi * tq) // tk

    def masked(s):
        key = n_full * tk + lax.broadcasted_iota(jnp.int32, (tk, lanes), 0)
        tok = qi * tq + lax.rem(lax.broadcasted_iota(jnp.int32, (tk, lanes), 1), tq)
        return jnp.where(key <= tok, s, -jnp.inf)

    s_ref[...] = scores(0)

    def pair(jj, carry):
        j = 2 * jj
        s_b = scores(j + 1)
        update(j, s_ref[...])
        s_ref[...] = scores(j + 2)
        update(j + 1, s_b)
        return carry

    n_pairs = n_full // 2
    lax.fori_loop(0, n_pairs, pair, 0)

    @pl.when(n_full % 2 == 1)
    def _():
        s_d = scores(n_full)
        update(n_full - 1, s_ref[...])
        update(n_full, masked(s_d))

    @pl.when(n_full % 2 == 0)
    def _():
        update(n_full, masked(s_ref[...]))

    for hh in range(heads):
        cols = slice(hh * tq, (hh + 1) * tq)
        o_t = acc_ref[:, cols] / l_ref[:, cols]
        o_ref[hh] = o_t.T.astype(o_ref.dtype)


def _mla_flash(qcat, kcat, ckvt, *, tq, tk):
    batch, heads, seq, kdim = qcat.shape
    kv_lora = ckvt.shape[2]
    nk = seq // tk
    kern = functools.partial(_mla_flash_kernel, heads=heads, tq=tq, tk=tk)
    lanes = heads * tq
    return pl.pallas_call(
        kern,
        grid=(batch, seq // tq),
        in_specs=[
            pl.BlockSpec((None, heads, tq, kdim), lambda b, i: (b, 0, i, 0)),
            pl.BlockSpec((None, nk, tk, kdim), lambda b, i: (b, 0, 0, 0),
                         pipeline_mode=pl.Buffered(1)),
            pl.BlockSpec((None, nk, kv_lora, tk), lambda b, i: (b, 0, 0, 0),
                         pipeline_mode=pl.Buffered(1)),
        ],
        out_specs=pl.BlockSpec((None, heads, tq, kv_lora), lambda b, i: (b, 0, i, 0)),
        out_shape=jax.ShapeDtypeStruct((batch, heads, seq, kv_lora), BF16),
        scratch_shapes=[pltpu.VMEM((1, lanes), F32), pltpu.VMEM((1, lanes), F32),
                        pltpu.VMEM((kv_lora, lanes), F32), pltpu.VMEM((tk, lanes), F32)],
        compiler_params=_params("arbitrary", "arbitrary"),
        name="mla_flash",
    )(qcat, kcat.reshape(batch, nk, tk, kdim), ckvt)


def _mla_decode_kernel(pt_ref, q_ref, knew_ref, ckv_hbm, kpe_hbm, o_ref, cbuf, rbuf, sem,
                       s_ref, *, layer, n_pages, page, kv_lora, chunk_pages):
    b = pl.program_id(0)
    nb = pl.num_programs(0)
    slot = lax.rem(b, 2)

    def page_copies(seq, slot_, p):
        pg = pt_ref[seq * n_pages + p]
        c = p // chunk_pages
        return (pltpu.make_async_copy(ckv_hbm.at[layer, pg], cbuf.at[slot_, p],
                                      sem.at[0, slot_, c]),
                pltpu.make_async_copy(kpe_hbm.at[layer, pg], rbuf.at[slot_, p],
                                      sem.at[1, slot_, c]))

    @pl.when(b == 0)
    def _():
        def body(p, carry):
            for c in page_copies(0, 0, p):
                c.start()
            return carry
        lax.fori_loop(0, n_pages, body, 0)

    q = q_ref[0]
    q_lat = q[:, :kv_lora]
    q_pe = q[:, kv_lora:]
    ck = chunk_pages * page
    n_chunks = n_pages // chunk_pages

    def wait_chunk(c):
        for p in range(c * chunk_pages, (c + 1) * chunk_pages):
            for cp in page_copies(b, slot, p):
                cp.wait()

    def prefetch_chunk(c):
        @pl.when(b + 1 < nb)
        def _():
            for p in range(c * chunk_pages, (c + 1) * chunk_pages):
                for cp in page_copies(b + 1, 1 - slot, p):
                    cp.start()

    def ckv_chunk(c):
        lo = c * chunk_pages
        return cbuf[slot, lo:lo + chunk_pages].reshape(ck, kv_lora).astype(BF16)

    def kpe_t_chunk(c):
        lo = c * chunk_pages
        return jnp.concatenate([rbuf[slot, lo + i] for i in range(chunk_pages)],
                               axis=-1).astype(BF16)

    for c in range(n_chunks):
        wait_chunk(c)
        s_ref[:, c * ck:(c + 1) * ck] = _dot_nt(q_lat, ckv_chunk(c)) + _dot(q_pe, kpe_t_chunk(c))
        prefetch_chunk(c)

    knew = knew_ref[0]
    s_new = jnp.sum(q.astype(F32) * knew.astype(F32), axis=-1, keepdims=True)
    s_all = s_ref[...]
    m = jnp.maximum(jnp.max(s_all, axis=-1, keepdims=True), s_new)
    p_all = jnp.exp2(s_all - m)
    p_new = jnp.exp2(s_new - m)
    denom = jnp.sum(p_all, axis=-1, keepdims=True) + p_new
    s_ref[...] = p_all

    acc = p_new.astype(BF16).astype(F32) * knew[:, :kv_lora].astype(F32)
    for c in range(n_chunks):
        acc = acc + _dot(s_ref[:, c * ck:(c + 1) * ck].astype(BF16), ckv_chunk(c))
    o_ref[0] = (acc / denom).astype(o_ref.dtype)


def _mla_decode(page_table_flat, q_s, knew, cache_ckv, cache_kpe_t, *, layer, chunk_pages=16):
    n_seq, heads, kdim = q_s.shape
    _, _, page, kv_lora = cache_ckv.shape
    rope = cache_kpe_t.shape[2]
    n_pages = page_table_flat.shape[0] // n_seq
    chunk_pages = math.gcd(chunk_pages, n_pages)
    kern = functools.partial(_mla_decode_kernel, layer=layer, n_pages=n_pages, page=page,
                             kv_lora=kv_lora, chunk_pages=chunk_pages)
    grid_spec = pltpu.PrefetchScalarGridSpec(
        num_scalar_prefetch=1,
        grid=(n_seq,),
        in_specs=[
            pl.BlockSpec((1, heads, kdim), lambda b, pt: (b, 0, 0)),
            pl.BlockSpec((1, 1, kdim), lambda b, pt: (b, 0, 0)),
            pl.BlockSpec(memory_space=pl.ANY),
            pl.BlockSpec(memory_space=pl.ANY),
        ],
        out_specs=pl.BlockSpec((1, heads, kv_lora), lambda b, pt: (b, 0, 0)),
        scratch_shapes=[
            pltpu.VMEM((2, n_pages, page, kv_lora), F32),
            pltpu.VMEM((2, n_pages, rope, page), F32),
            pltpu.SemaphoreType.DMA((2, 2, n_pages // chunk_pages)),
            pltpu.VMEM((heads, n_pages * page), F32),
        ],
    )
    return pl.pallas_call(
        kern,
        grid_spec=grid_spec,
        out_shape=jax.ShapeDtypeStruct((n_seq, heads, kv_lora), BF16),
        compiler_params=_params("arbitrary"),
        name="mla_decode",
    )(page_table_flat, q_s, knew, cache_ckv, cache_kpe_t)


def _mla_out_kernel(o_ref, x_ref, wuv_ref, wo_ref, out_ref, *, heads):
    parts = [_dot(o_ref[hh], wuv_ref[hh]).astype(BF16) for hh in range(heads)]
    o = jnp.concatenate(parts, axis=-1)
    out_ref[...] = x_ref[...] + _dot(o, wo_ref[...])


def _mla_out(o_lat, x2d, wuv, wo, *, tm):
    batch, heads, seq, kv_lora = o_lat.shape
    d_model = x2d.shape[1]
    nt = seq // tm
    return pl.pallas_call(
        functools.partial(_mla_out_kernel, heads=heads),
        grid=(batch * nt,),
        in_specs=[
            pl.BlockSpec((None, heads, tm, kv_lora), lambda i: (i // nt, 0, i % nt, 0)),
            pl.BlockSpec((tm, d_model), lambda i: (i, 0)),
            _const_spec(wuv.shape), _const_spec(wo.shape),
        ],
        out_specs=pl.BlockSpec((tm, d_model), lambda i: (i, 0)),
        out_shape=jax.ShapeDtypeStruct(x2d.shape, F32),
        compiler_params=_params("arbitrary"),
        name="mla_out",
    )(o_lat, x2d, wuv, wo)


def _sb_proj_kernel(x_ref, gmix_ref, w_ref, q_ref, kbf_ref, vbf_ref, k_ref, v_ref, *,
                    heads, head_dim):
    scale = math.log2(math.e) / math.sqrt(head_dim)
    h = _rmsnorm(x_ref[...], gmix_ref[...]).astype(BF16)
    y = _dot(h, w_ref[...])
    for hh in range(heads):
        q_ref[hh] = (y[:, hh * head_dim:(hh + 1) * head_dim] * scale).astype(BF16)
    nq = heads * head_dim
    nk = kbf_ref.shape[-1]
    k = y[:, nq:nq + nk]
    v = y[:, nq + nk:]
    kv = nk // head_dim
    tm = k.shape[0]
    for kh in range(kv):
        cols = slice(kh * head_dim, (kh + 1) * head_dim)
        k_ref[pl.ds(kh, tm, stride=kv), :] = k[:, cols]
        v_ref[pl.ds(kh, tm, stride=kv), :] = v[:, cols]
    kbf_ref[...] = k.astype(BF16)
    vbf_ref[...] = v.astype(BF16)


def _sb_project(x2d, gmix, w, *, batch, seq, tm, heads, head_dim):
    d_model = x2d.shape[1]
    nk = (w.shape[1] - heads * head_dim) // 2
    kv = nk // head_dim
    nt = seq // tm
    rows = batch * seq
    return pl.pallas_call(
        functools.partial(_sb_proj_kernel, heads=heads, head_dim=head_dim),
        grid=(batch * nt,),
        in_specs=[pl.BlockSpec((tm, d_model), lambda i: (i, 0)),
                  _const_spec(gmix.shape), _const_spec(w.shape)],
        out_specs=[
            pl.BlockSpec((None, heads, tm, head_dim), lambda i: (i // nt, 0, i % nt, 0)),
            pl.BlockSpec((None, tm, nk), lambda i: (i // nt, i % nt, 0)),
            pl.BlockSpec((None, tm, nk), lambda i: (i // nt, i % nt, 0)),
            pl.BlockSpec((tm * kv, head_dim), lambda i: (i, 0)),
            pl.BlockSpec((tm * kv, head_dim), lambda i: (i, 0)),
        ],
        out_shape=[
            jax.ShapeDtypeStruct((batch, heads, seq, head_dim), BF16),
            jax.ShapeDtypeStruct((batch, seq, nk), BF16),
            jax.ShapeDtypeStruct((batch, seq, nk), BF16),
            jax.ShapeDtypeStruct((rows * kv, head_dim), F32),
            jax.ShapeDtypeStruct((rows * kv, head_dim), F32),
        ],
        compiler_params=_params("arbitrary"),
        name="sb_project",
    )(x2d, gmix, w)


def _sb_block(q, k, v, tri, carry, mask):
    z = _dot_nt(q, k)
    sp = jnp.log(1.0 + jnp.exp2(-jnp.abs(z))) * math.log2(math.e)
    log_beta = jnp.minimum(z, 0.0) - sp
    log_keep = jnp.minimum(-z, 0.0) - sp
    if mask is not None:
        log_keep = jnp.where(mask, log_keep, 0.0)
    hi = log_keep.astype(BF16)
    lo = (log_keep - hi.astype(F32)).astype(BF16)
    suffix = _dot(hi, tri) + _dot(lo, tri)
    a = jnp.exp2(log_beta + suffix + carry)
    if mask is not None:
        a = jnp.where(mask, a, 0.0)
    return _dot(a.astype(BF16), v), jnp.sum(log_keep, axis=-1, keepdims=True)


def _sb_attn_kernel(q_ref, k_ref, v_ref, tri_ref, o_ref, carry_ref, acc_ref, *, group, tq,
                    head_dim):
    qi = pl.program_id(2)
    rows = group * tq
    q = q_ref[...].reshape(rows, head_dim)
    tri = tri_ref[...]

    row = lax.broadcasted_iota(jnp.int32, (tq, tq), 0)
    col = lax.broadcasted_iota(jnp.int32, (tq, tq), 1)
    mask = jnp.broadcast_to((col < row)[None], (group, tq, tq)).reshape(rows, tq)
    start = pl.multiple_of(qi * tq, tq)
    out, ksum = _sb_block(q, k_ref[pl.ds(start, tq), :], v_ref[pl.ds(start, tq), :], tri,
                          jnp.zeros((rows, 1), F32), mask)
    acc_ref[...] = out
    carry_ref[...] = ksum

    def cond(state):
        j, cmax = state
        return jnp.logical_and(j >= 0, cmax > SB_UNDERFLOW)

    def body(state):
        j, _ = state
        st = pl.multiple_of(j * tq, tq)
        carry = carry_ref[...]
        out, ksum = _sb_block(q, k_ref[pl.ds(st, tq), :], v_ref[pl.ds(st, tq), :], tri, carry,
                              None)
        acc_ref[...] += out
        carry = carry + ksum
        carry_ref[...] = carry
        return j - 1, jnp.max(carry)

    lax.while_loop(cond, body, (qi - 1, jnp.max(ksum)))
    o_ref[...] = acc_ref[...].reshape(group, tq, head_dim).astype(o_ref.dtype)


def _sb_attention(q, kbf, vbf, tri, *, tq):
    batch, heads, seq, head_dim = q.shape
    kv_heads = kbf.shape[-1] // head_dim
    group = heads // kv_heads
    rows = group * tq
    kern = functools.partial(_sb_attn_kernel, group=group, tq=tq, head_dim=head_dim)
    return pl.pallas_call(
        kern,
        grid=(batch, kv_heads, seq // tq),
        in_specs=[
            pl.BlockSpec((None, group, tq, head_dim), lambda b, h, i: (b, h, i, 0)),
            pl.BlockSpec((None, seq, head_dim), lambda b, h, i: (b, 0, h)),
            pl.BlockSpec((None, seq, head_dim), lambda b, h, i: (b, 0, h)),
            _const_spec(tri.shape),
        ],
        out_specs=pl.BlockSpec((None, group, tq, head_dim), lambda b, h, i: (b, h, i, 0)),
        out_shape=jax.ShapeDtypeStruct(q.shape, BF16),
        scratch_shapes=[pltpu.VMEM((rows, 1), F32), pltpu.VMEM((rows, head_dim), F32)],
        compiler_params=_params("arbitrary", "arbitrary", "arbitrary"),
        name="sb_attention",
    )(q, kbf, vbf, tri)


def _sb_decode_kernel(pt_ref, q_ref, tri_ref, k_hbm, v_hbm, o_ref, kbuf, vbuf, sem, *, layer,
                      n_pages, kv_heads, group, head_dim):
    b = pl.program_id(0)
    nb = pl.num_programs(0)
    bset = lax.rem(b, 2)
    heads = kv_heads * group
    newest = n_pages - 1

    def slot_of(p):
        return lax.rem(newest - p, 2)

    def copies(seq, set_, p):
        pg = pt_ref[seq * n_pages + p]
        slot_ = slot_of(p)
        return (pltpu.make_async_copy(k_hbm.at[layer, pg], kbuf.at[set_, slot_],
                                      sem.at[0, set_, slot_]),
                pltpu.make_async_copy(v_hbm.at[layer, pg], vbuf.at[set_, slot_],
                                      sem.at[1, set_, slot_]))

    def start_newest(seq, set_):
        for p in range(newest, max(newest - 2, -1), -1):
            for c in copies(seq, set_, p):
                c.start()

    @pl.when(b == 0)
    def _():
        start_newest(0, 0)

    @pl.when(b + 1 < nb)
    def _():
        start_newest(b + 1, 1 - bset)

    q = q_ref[0]
    tri = tri_ref[...]
    width = kbuf.shape[2]
    kv_of_row = lax.broadcasted_iota(jnp.int32, (heads, width), 0) // group
    kv_of_col = lax.rem(lax.broadcasted_iota(jnp.int32, (heads, width), 1), kv_heads)
    mask = kv_of_row == kv_of_col

    def cond(state):
        p, cmax, _, _ = state
        return jnp.logical_and(p >= 0, cmax > SB_UNDERFLOW)

    def body(state):
        p, _, carry, acc = state
        slot_ = slot_of(p)
        for c in copies(b, bset, p):
            c.wait()
        out, ksum = _sb_block(q, kbuf[bset, slot_].astype(BF16), vbuf[bset, slot_].astype(BF16),
                              tri, carry, mask)
        carry = carry + ksum
        cmax = jnp.max(carry)

        @pl.when(jnp.logical_and(p >= 2, cmax > SB_UNDERFLOW))
        def _():
            for c in copies(b, bset, p - 2):
                c.start()

        return p - 1, cmax, carry, acc + out

    init = (jnp.int32(newest), jnp.float32(0.0), jnp.zeros((heads, 1), F32),
            jnp.zeros((heads, head_dim), F32))
    p_end, _, _, acc = lax.while_loop(cond, body, init)

    @pl.when(p_end >= 0)
    def _():
        for c in copies(b, bset, p_end):
            c.wait()

    o_ref[0] = acc.astype(o_ref.dtype)


def _sb_decode(page_table_flat, q_s, tri, cache_k, cache_v, *, layer, kv_heads):
    n_seq, heads, head_dim = q_s.shape
    width = cache_k.shape[2]
    n_pages = page_table_flat.shape[0] // n_seq
    kern = functools.partial(_sb_decode_kernel, layer=layer, n_pages=n_pages, kv_heads=kv_heads,
                             group=heads // kv_heads, head_dim=head_dim)
    grid_spec = pltpu.PrefetchScalarGridSpec(
        num_scalar_prefetch=1,
        grid=(n_seq,),
        in_specs=[
            pl.BlockSpec((1, heads, head_dim), lambda b, pt: (b, 0, 0)),
            pl.BlockSpec(tri.shape, lambda b, pt: (0, 0)),
            pl.BlockSpec(memory_space=pl.ANY),
            pl.BlockSpec(memory_space=pl.ANY),
        ],
        out_specs=pl.BlockSpec((1, heads, head_dim), lambda b, pt: (b, 0, 0)),
        scratch_shapes=[
            pltpu.VMEM((2, 2, width, head_dim), F32),
            pltpu.VMEM((2, 2, width, head_dim), F32),
            pltpu.SemaphoreType.DMA((2, 2, 2)),
        ],
    )
    return pl.pallas_call(
        kern,
        grid_spec=grid_spec,
        out_shape=jax.ShapeDtypeStruct((n_seq, heads, head_dim), BF16),
        compiler_params=_params("arbitrary"),
        name="sb_decode",
    )(page_table_flat, q_s, tri, cache_k, cache_v)


def _sb_out_kernel(o_ref, x_ref, wo_ref, out_ref, *, heads):
    o = jnp.concatenate([o_ref[hh] for hh in range(heads)], axis=-1)
    out_ref[...] = x_ref[...] + _dot(o, wo_ref[...])


def _sb_out(o, x2d, wo, *, tm):
    batch, heads, seq, head_dim = o.shape
    d_model = x2d.shape[1]
    nt = seq // tm
    return pl.pallas_call(
        functools.partial(_sb_out_kernel, heads=heads),
        grid=(batch * nt,),
        in_specs=[
            pl.BlockSpec((None, heads, tm, head_dim), lambda i: (i // nt, 0, i % nt, 0)),
            pl.BlockSpec((tm, d_model), lambda i: (i, 0)),
            _const_spec(wo.shape),
        ],
        out_specs=pl.BlockSpec((tm, d_model), lambda i: (i, 0)),
        out_shape=jax.ShapeDtypeStruct(x2d.shape, F32),
        compiler_params=_params("arbitrary"),
        name="sb_out",
    )(o, x2d, wo)


def _ffn_gate(g2, g1, g, u, cw_ref, cb_ref, cols):
    w0 = cw_ref[0:1, cols]
    w1 = cw_ref[1:2, cols]
    w2 = cw_ref[2:3, cols]
    gc = cb_ref[:, cols] + ((g2 * w0 + g1 * w1) + g * w2)
    return (gc * jax.nn.sigmoid(gc) * u).astype(BF16)


def _layer_spec(shape, layer):
    nd = len(shape)
    return pl.BlockSpec((None,) + tuple(shape[1:]), lambda *_: (layer,) + (0,) * (nd - 1),
                        pipeline_mode=pl.Buffered(1))


def _ffn_finish(res, gfin_ref, final):
    return _rmsnorm(res, gfin_ref[...]) if final else res


def _ffn_prompt_kernel(x_ref, gn_ref, wup_ref, cw_ref, cb_ref, wdn_ref, gfin_ref, out_ref,
                       tail_ref, act_ref, carry_ref, *, d_ff, tf, tiles_per_seq, final):
    i = pl.program_id(0)
    x = x_ref[...]
    tm = x.shape[0]
    h = _rmsnorm(x, gn_ref[...]).astype(BF16)

    @pl.when(i % tiles_per_seq == 0)
    def _():
        carry_ref[...] = jnp.zeros(carry_ref.shape, F32)

    row = lax.broadcasted_iota(jnp.int32, (SUBLANE, tf), 0)
    for c in range(d_ff // tf):
        cols = slice(c * tf, (c + 1) * tf)
        g = _dot(h, wup_ref[:, cols])
        u = _dot(h, wup_ref[:, d_ff + c * tf:d_ff + (c + 1) * tf])
        prev = carry_ref[:, cols]
        r1 = pltpu.roll(g, 1, 0)
        r2 = pltpu.roll(g, 2, 0)
        head1 = jnp.where(row < 1, pltpu.roll(prev, 1, 0), r1[:SUBLANE])
        head2 = jnp.where(row < 2, pltpu.roll(prev, 2, 0), r2[:SUBLANE])
        g1 = jnp.concatenate([head1, r1[SUBLANE:]], axis=0)
        g2 = jnp.concatenate([head2, r2[SUBLANE:]], axis=0)
        tail = g[tm - SUBLANE:, :]
        carry_ref[:, cols] = tail
        tail_ref[:, cols] = tail
        act_ref[:, cols] = _ffn_gate(g2, g1, g, u, cw_ref, cb_ref, cols)
    out_ref[...] = _ffn_finish(x + _dot(act_ref[...], wdn_ref[...]), gfin_ref, final)


def _ffn_prompt(x2d, gn, wup, cw, cb, wdn, gfin, *, layer, final, batch, seq, tm, tf):
    d_model = x2d.shape[1]
    d_ff = wdn.shape[1]
    nt = seq // tm
    kern = functools.partial(_ffn_prompt_kernel, d_ff=d_ff, tf=tf, tiles_per_seq=nt,
                             final=final)
    return pl.pallas_call(
        kern,
        grid=(batch * nt,),
        in_specs=[pl.BlockSpec((tm, d_model), lambda i: (i, 0)),
                  _const_spec(gn.shape), _layer_spec(wup.shape, layer), _const_spec(cw.shape),
                  _const_spec(cb.shape), _layer_spec(wdn.shape, layer),
                  _const_spec(gfin.shape)],
        out_specs=[pl.BlockSpec((tm, d_model), lambda i: (i, 0)),
                   pl.BlockSpec((None, SUBLANE, d_ff), lambda i: (i // nt, 0, 0))],
        out_shape=[jax.ShapeDtypeStruct(x2d.shape, F32),
                   jax.ShapeDtypeStruct((batch, SUBLANE, d_ff), F32)],
        scratch_shapes=[pltpu.VMEM((tm, d_ff), BF16), pltpu.VMEM((SUBLANE, d_ff), F32)],
        compiler_params=_params("arbitrary"),
        name="ffn_prompt",
    )(x2d, gn, wup, cw, cb, wdn, gfin)


def _ffn_sample_kernel(x_ref, s0_ref, s1_ref, gn_ref, wup_ref, cw_ref, cb_ref, wdn_ref,
                       gfin_ref, out_ref, gate_ref, act_ref, *, d_ff, tf, final):
    x = x_ref[...]
    h = _rmsnorm(x, gn_ref[...]).astype(BF16)
    for c in range(d_ff // tf):
        cols = slice(c * tf, (c + 1) * tf)
        g = _dot(h, wup_ref[:, cols])
        u = _dot(h, wup_ref[:, d_ff + c * tf:d_ff + (c + 1) * tf])
        gate_ref[:, cols] = g
        act_ref[:, cols] = _ffn_gate(s0_ref[:, cols], s1_ref[:, cols], g, u, cw_ref, cb_ref, cols)
    out_ref[...] = _ffn_finish(x + _dot(act_ref[...], wdn_ref[...]), gfin_ref, final)


def _ffn_sample(x2d, s0, s1, gn, wup, cw, cb, wdn, gfin, *, layer, final, tf):
    rows, d_model = x2d.shape
    d_ff = wdn.shape[1]
    kern = functools.partial(_ffn_sample_kernel, d_ff=d_ff, tf=tf, final=final)
    return pl.pallas_call(
        kern,
        grid=(1,),
        in_specs=[_const_spec(x2d.shape), _const_spec(s0.shape), _const_spec(s1.shape),
                  _const_spec(gn.shape), _layer_spec(wup.shape, layer), _const_spec(cw.shape),
                  _const_spec(cb.shape), _layer_spec(wdn.shape, layer),
                  _const_spec(gfin.shape)],
        out_specs=[pl.BlockSpec((rows, d_model), lambda i: (0, 0)),
                   pl.BlockSpec((rows, d_ff), lambda i: (0, 0))],
        out_shape=[jax.ShapeDtypeStruct(x2d.shape, F32),
                   jax.ShapeDtypeStruct((rows, d_ff), F32)],
        scratch_shapes=[pltpu.VMEM((rows, d_ff), BF16)],
        compiler_params=_params("arbitrary"),
        name="ffn_sample",
    )(x2d, s0, s1, gn, wup, cw, cb, wdn, gfin)


def _tiles(seq, d_ff):
    def fit(pref):
        t = min(pref, seq)
        while seq % t:
            t //= 2
        return t
    tf = 2 * LANE if d_ff % (2 * LANE) == 0 else LANE
    return dict(tm=fit(512), mla_tq=fit(512), sb_tq=fit(256), tf=tf)


def _rot_cols(w):
    half = w.shape[-1] // 2
    return jnp.concatenate([-w[..., half:], w[..., :half]], axis=-1)


def _rope_tables(pos, rope):
    half = rope // 2
    inv = ROPE_BASE ** (-jnp.arange(half, dtype=F32) / half)
    ang = pos.astype(F32)[:, None] * inv[None, :]
    reps = LANE // half
    return jnp.tile(jnp.cos(ang), (1, reps)), jnp.tile(jnp.sin(ang), (1, reps))


def kernel(x_prompt, x_sample, cache_mla_ckv, cache_mla_kpe, cache_sb_k, cache_sb_v, state_ffn_conv, page_table, norm_mix, norm_ffn, norm_final, mla_w_dq, mla_g_q, mla_w_uq, mla_w_dkv, mla_g_kv, mla_w_uk, mla_w_uv, mla_w_o, sb_w_qkv, sb_w_o, ffn_w_up, ffn_conv_w, ffn_conv_b, ffn_w_down):
    batch, seq, d_model = x_prompt.shape
    n_seq, dec_seq, _ = x_sample.shape
    assert dec_seq == 1, "the sample group decodes one token per sequence"
    depth = norm_mix.shape[0]
    d_ff = ffn_w_down.shape[1]
    n_pages, page = page_table.shape[1], cache_mla_ckv.shape[2]
    past = n_pages * page
    heads, nope = mla_w_uk.shape[2], mla_w_uk.shape[3]
    q_lora, kv_lora = mla_w_dq.shape[2], mla_w_uk.shape[1]
    rope = mla_w_uq.shape[3] - nope
    sb_kv, sb_dim = cache_sb_k.shape[3], cache_sb_k.shape[4]
    sb_heads = sb_w_o.shape[1] // sb_dim
    t = _tiles(seq, d_ff)

    cos_p, sin_p = _rope_tables(jnp.arange(seq, dtype=jnp.int32), rope)
    cos_s, sin_s = _rope_tables(past + jnp.arange(dec_seq, dtype=jnp.int32), rope)
    cos_s = jnp.broadcast_to(cos_s, (n_seq, LANE))
    sin_s = jnp.broadcast_to(sin_s, (n_seq, LANE))
    pt_flat = page_table.reshape(-1)
    cache_k2 = cache_sb_k.reshape(cache_sb_k.shape[:2] + (page * sb_kv, sb_dim))
    cache_v2 = cache_sb_v.reshape(cache_sb_v.shape[:2] + (page * sb_kv, sb_dim))
    cache_kpe_t = jnp.swapaxes(cache_mla_kpe, 2, 3)
    tri_p = jnp.tril(jnp.ones((t["sb_tq"], t["sb_tq"]), BF16), k=-1)
    tri_s = jnp.tril(jnp.ones((page * sb_kv, page * sb_kv), BF16), k=-1)

    xp = x_prompt.reshape(batch * seq, d_model)
    xs = x_sample.reshape(n_seq, d_model)
    row = lambda v: v.reshape(1, -1)
    wup_all = ffn_w_up.astype(BF16)
    wdn_all = ffn_w_down.astype(BF16)
    gfin = row(norm_final)
    outs = {k: [] for k in ("ckv_p", "kpe_p", "ckv_s", "kpe_s", "sbk_p", "sbv_p", "sbk_s",
                            "sbv_s", "conv_p", "conv_s")}

    for i in range(depth):
        j = i // 2
        gmix = row(norm_mix[i])
        if i % 2 == 0:
            pad = jnp.zeros((d_model, LANE - rope), F32)
            w_r = mla_w_dkv[j][:, kv_lora:]
            w1 = jnp.concatenate([mla_w_dq[j], mla_w_dkv[j][:, :kv_lora], w_r, pad,
                                  _rot_cols(w_r), pad], axis=1).astype(BF16)
            wq = mla_w_uq[j]
            w2 = jnp.concatenate([wq[:, :, :nope].reshape(q_lora, heads * nope),
                                  wq[:, :, nope:].reshape(q_lora, heads * rope),
                                  _rot_cols(wq[:, :, nope:]).reshape(q_lora, heads * rope)],
                                 axis=1).astype(BF16)
            wukt = jnp.transpose(mla_w_uk[j], (1, 2, 0)).astype(BF16)
            wuv = jnp.transpose(mla_w_uv[j], (1, 0, 2)).astype(BF16)
            wo = mla_w_o[j].astype(BF16)
            prm = (gmix, w1, row(mla_g_q[j]), row(mla_g_kv[j]), w2, wukt)

            qcat, kcat, ckvt, ckv_p, kpe_p = _mla_project(xp, cos_p, sin_p, *prm, batch=batch,
                                                          seq=seq, tm=t["tm"])
            o_p = _mla_flash(qcat, kcat, ckvt, tq=t["mla_tq"], tk=t["tm"])
            xp = _mla_out(o_p, xp, wuv, wo, tm=t["tm"])

            qcat_s, kcat_s, _, ckv_s, kpe_s = _mla_project(xs, cos_s, sin_s, *prm, batch=1,
                                                           seq=n_seq, tm=n_seq)
            q_s = jnp.transpose(qcat_s[0], (1, 0, 2))
            o_s = _mla_decode(pt_flat, q_s, kcat_s.reshape(n_seq, 1, -1), cache_mla_ckv,
                              cache_kpe_t, layer=j)
            xs = _mla_out(jnp.transpose(o_s, (1, 0, 2))[None], xs, wuv, wo, tm=n_seq)

            outs["ckv_p"].append(ckv_p.reshape(batch, seq, kv_lora))
            outs["kpe_p"].append(kpe_p.reshape(batch, seq, rope))
            outs["ckv_s"].append(ckv_s.reshape(n_seq, dec_seq, kv_lora))
            outs["kpe_s"].append(kpe_s.reshape(n_seq, dec_seq, rope))
        else:
            wqkv = sb_w_qkv[j].astype(BF16)
            wo = sb_w_o[j].astype(BF16)
            q_p, kbf, vbf, k_p, v_p = _sb_project(xp, gmix, wqkv, batch=batch, seq=seq,
                                                  tm=t["tm"], heads=sb_heads, head_dim=sb_dim)
            o_p = _sb_attention(q_p, kbf, vbf, tri_p, tq=t["sb_tq"])
            xp = _sb_out(o_p, xp, wo, tm=t["tm"])

            q_s, _, _, k_s, v_s = _sb_project(xs, gmix, wqkv, batch=1, seq=n_seq, tm=n_seq,
                                              heads=sb_heads, head_dim=sb_dim)
            o_s = _sb_decode(pt_flat, jnp.transpose(q_s[0], (1, 0, 2)), tri_s, cache_k2,
                             cache_v2, layer=j, kv_heads=sb_kv)
            xs = _sb_out(jnp.transpose(o_s, (1, 0, 2))[None], xs, wo, tm=n_seq)

            outs["sbk_p"].append(k_p.reshape(batch, seq, sb_kv, sb_dim))
            outs["sbv_p"].append(v_p.reshape(batch, seq, sb_kv, sb_dim))
            outs["sbk_s"].append(k_s.reshape(n_seq, dec_seq, sb_kv, sb_dim))
            outs["sbv_s"].append(v_s.reshape(n_seq, dec_seq, sb_kv, sb_dim))

        gn = row(norm_ffn[i])
        cw, cb = ffn_conv_w[i], row(ffn_conv_b[i])
        ffn = dict(layer=i, final=(i == depth - 1), tf=t["tf"])
        xp, tail = _ffn_prompt(xp, gn, wup_all, cw, cb, wdn_all, gfin, batch=batch, seq=seq,
                               tm=t["tm"], **ffn)
        st = state_ffn_conv[i]
        xs, gate_s = _ffn_sample(xs, st[:, 0], st[:, 1], gn, wup_all, cw, cb, wdn_all, gfin,
                                 **ffn)
        outs["conv_p"].append(tail[:, SUBLANE - 2:])
        outs["conv_s"].append(jnp.stack([st[:, 1], gate_s], axis=1))

    y_p = xp.reshape(batch, seq, d_model)
    y_s = xs.reshape(n_seq, dec_seq, d_model)
    st = lambda k: jnp.stack(outs[k])
    return (y_p, y_s, st("ckv_p"), st("kpe_p"), st("sbk_p"), st("sbv_p"), st("conv_p"),
            st("ckv_s"), st("kpe_s"), st("sbk_s"), st("sbv_s"), st("conv_s"))
```
